```python
import jax, jax.numpy as jnp
from jax import lax
import numpy as np

D_MODEL = 2048
BATCH = 1
SEQ = 8192
DEPTH = 2

N_A_LAYERS = DEPTH // 2
N_B_LAYERS = DEPTH - N_A_LAYERS
HEAD_DIM = 128
ROT_DIM = HEAD_DIM // 4
ROPE_THETA = 500000.0
MEM_HEADS = 4
MEM_DIM = MEM_HEADS * HEAD_DIM
N_MEM = 256
CONV_CH = 3 * D_MODEL // 4
CONV_K = 3
NSA_HEADS = (D_MODEL - MEM_DIM) // HEAD_DIM
NSA_KV_GROUPS = 4
NSA_REP = NSA_HEADS // NSA_KV_GROUPS
KV_WIDTH = NSA_KV_GROUPS * HEAD_DIM
CMP_STRIDE = 16
CMP_BLOCK = 2 * CMP_STRIDE
SLC_BLOCK = 64
N_SELECT = 16
WINDOW = 512
Q_BLOCK = 128
D_FF = 256 * ((8 * D_MODEL // 3 + 255) // 256)
RMS_EPS = 1e-6

kernel_name = 'yoco_shortconv_nsa_macaron_hybrid'


def rms_norm(x, g):
    xf = x.astype(jnp.float32)
    y = xf * lax.rsqrt(jnp.mean(xf * xf, axis=-1, keepdims=True) + RMS_EPS)
    return (y * g.astype(jnp.float32)).astype(x.dtype)


def swiglu(x, w_gate, w_up, w_down):
    return (jax.nn.silu(x @ w_gate) * (x @ w_up)) @ w_down


def _heads(x, n):
    b, s, _ = x.shape
    return x.reshape(b, s, n, HEAD_DIM).transpose(0, 2, 1, 3)


def _merge(x):
    b, n, s, d = x.shape
    return x.transpose(0, 2, 1, 3).reshape(b, s, n * d)


def rotary(x, pos):
    half = ROT_DIM // 2
    inv = 1.0 / (ROPE_THETA ** (jnp.arange(half, dtype=jnp.float32) / half))
    ang = pos.astype(jnp.float32)[:, None, :, None] * inv
    cos = jnp.cos(ang).astype(x.dtype)
    sin = jnp.sin(ang).astype(x.dtype)
    x1 = x[..., :half]
    x2 = x[..., half:ROT_DIM]
    return jnp.concatenate([x1 * cos - x2 * sin, x2 * cos + x1 * sin, x[..., ROT_DIM:]], axis=-1)


def masked_softmax(s, mask):
    s = jnp.where(mask, s, -jnp.inf)
    m = jnp.max(s, axis=-1, keepdims=True)
    m = jnp.where(jnp.isfinite(m), m, 0.0)
    e = jnp.where(mask, jnp.exp(s - m), 0.0)
    d = jnp.sum(e, axis=-1, keepdims=True)
    return e / jnp.where(d > 0, d, 1.0)


def short_conv(u, w):
    return lax.conv_general_dilated(u, w[:, None, :], window_strides=(1,), padding=[(CONV_K - 1, 0)],
                                    dimension_numbers=('NWC', 'WIO', 'NWC'),
                                    feature_group_count=u.shape[-1])


def memory_attention(q_in, mem, mem_norm_g, w_mem_kv):
    kv = rms_norm(mem, mem_norm_g) @ w_mem_kv
    k, v = jnp.split(kv, 2, axis=-1)
    k = _heads(k, MEM_HEADS)
    v = _heads(v, MEM_HEADS)
    q = _heads(q_in, MEM_HEADS)
    s = jnp.einsum('bhqd,bhkd->bhqk', q, k).astype(jnp.float32) * (HEAD_DIM ** -0.5)
    p = jax.nn.softmax(s, axis=-1).astype(v.dtype)
    return _merge(jnp.einsum('bhqk,bhkd->bhqd', p, v))


def _compress(k, pos_emb, w1, w2):
    b, g, s, d = k.shape
    chunks = k.reshape(b, g, s // CMP_STRIDE, CMP_STRIDE, d)
    blocks = jnp.concatenate([chunks[:, :, :-1], chunks[:, :, 1:]], axis=3) + pos_emb
    flat = blocks.reshape(b, g, s // CMP_STRIDE - 1, CMP_BLOCK * d)
    return jax.nn.gelu(flat @ w1) @ w2


def _cmp_to_slc(n_cmp, n_slc):
    cs = jnp.arange(n_cmp)[:, None] * CMP_STRIDE
    ss = jnp.arange(n_slc)[None, :] * SLC_BLOCK
    ov = jnp.clip(jnp.minimum(cs + CMP_BLOCK, ss + SLC_BLOCK) - jnp.maximum(cs, ss), 0, None)
    return ov.astype(jnp.float32) / CMP_BLOCK


def shared_nsa_kv(h, positions, kv_norm, w_kv, cmp_pos_k, cmp_w1_k, cmp_w2_k, cmp_pos_v, cmp_w1_v, cmp_w2_v):
    u = rms_norm(h, kv_norm) @ w_kv
    kc, vc, ks, vs, kw, vw = [_heads(t, NSA_KV_GROUPS) for t in jnp.split(u, 6, axis=-1)]
    kc = _compress(rotary(kc, positions), cmp_pos_k, cmp_w1_k, cmp_w2_k)
    vc = _compress(vc, cmp_pos_v, cmp_w1_v, cmp_w2_v)
    return kc, vc, rotary(ks, positions), vs, rotary(kw, positions), vw


def nsa_attention(q, gates, kc, vc, ks, vs, kw, vw):
    b, n_h, s, d = q.shape
    g = NSA_KV_GROUPS
    n_cmp = kc.shape[2]
    n_slc = s // SLC_BLOCK
    n_sel = min(N_SELECT, n_slc)
    overlap = _cmp_to_slc(n_cmp, n_slc)
    ks_blk = ks.reshape(b, g, n_slc, SLC_BLOCK, d)
    vs_blk = vs.reshape(b, g, n_slc, SLC_BLOCK, d)
    pad = ((0, 0), (0, 0), (WINDOW, 0), (0, 0))
    kw_pad = jnp.pad(kw, pad)
    vw_pad = jnp.pad(vw, pad)
    cmp_end = jnp.arange(n_cmp) * CMP_STRIDE + CMP_BLOCK - 1
    blk_ids = jnp.arange(n_slc)
    b_idx = jnp.arange(b)[:, None, None, None]
    g_idx = jnp.arange(g)[None, :, None, None]
    scale = HEAD_DIM ** -0.5

    def one_block(start):
        t = start + jnp.arange(Q_BLOCK)
        qb = lax.dynamic_slice_in_dim(q, start, Q_BLOCK, axis=2).reshape(b, g, NSA_REP, Q_BLOCK, d)
        gb = lax.dynamic_slice_in_dim(gates, start, Q_BLOCK, axis=2).reshape(b, g, NSA_REP, Q_BLOCK, 3)
        s_c = jnp.einsum('bgrqd,bgcd->bgrqc', qb, kc).astype(jnp.float32) * scale
        p_c = masked_softmax(s_c, cmp_end[None, :] <= t[:, None])
        o_c = jnp.einsum('bgrqc,bgcd->bgrqd', p_c.astype(vc.dtype), vc)
        imp = jnp.einsum('bgrqc,cs->bgqs', p_c, overlap)
        jt = (t // SLC_BLOCK)[:, None]
        valid = blk_ids[None, :] <= jt
        forced = (blk_ids[None, :] == 0) | (blk_ids[None, :] == jt) | (blk_ids[None, :] == jt - 1)
        score = jnp.where(forced, jnp.inf, jnp.where(valid, imp, -jnp.inf))
        _, idx = lax.top_k(score, n_sel)
        k_sel = ks_blk[b_idx, g_idx, idx]
        v_sel = vs_blk[b_idx, g_idx, idx]
        tok = idx[..., None] * SLC_BLOCK + jnp.arange(SLC_BLOCK)
        sel_mask = (tok <= t[:, None, None])[:, :, None].reshape(b, g, 1, Q_BLOCK, n_sel * SLC_BLOCK)
        s_s = jnp.einsum('bgrqd,bgqkld->bgrqkl', qb, k_sel).astype(jnp.float32) * scale
        p_s = masked_softmax(s_s.reshape(b, g, NSA_REP, Q_BLOCK, n_sel * SLC_BLOCK), sel_mask)
        p_s = p_s.reshape(b, g, NSA_REP, Q_BLOCK, n_sel, SLC_BLOCK).astype(v_sel.dtype)
        o_s = jnp.einsum('bgrqkl,bgqkld->bgrqd', p_s, v_sel)
        kwb = lax.dynamic_slice_in_dim(kw_pad, start, Q_BLOCK + WINDOW, axis=2)
        vwb = lax.dynamic_slice_in_dim(vw_pad, start, Q_BLOCK + WINDOW, axis=2)
        kt = start - WINDOW + jnp.arange(Q_BLOCK + WINDOW)
        wmask = (kt[None, :] >= 0) & (kt[None, :] <= t[:, None]) & (kt[None, :] > t[:, None] - WINDOW)
        s_w = jnp.einsum('bgrqd,bgkd->bgrqk', qb, kwb).astype(jnp.float32) * scale
        p_w = masked_softmax(s_w, wmask).astype(vwb.dtype)
        o_w = jnp.einsum('bgrqk,bgkd->bgrqd', p_w, vwb)
        o = gb[..., 0:1] * o_c + gb[..., 1:2] * o_s + gb[..., 2:3] * o_w
        return o.reshape(b, n_h, Q_BLOCK, d)

    starts = jnp.arange(s // Q_BLOCK, dtype=jnp.int32) * Q_BLOCK
    out = lax.map(one_block, starts)
    return out.transpose(1, 0, 3, 2, 4).reshape(b, s, n_h * d)


def setup_inputs(seed: int = 0) -> dict:
    key = jax.random.key(seed)
    k = jax.random.split(key, 26)

    def nrm(kk, shape, fan_in):
        return jax.random.normal(kk, shape, jnp.float32) * fan_in ** -0.5

    def gain(kk, shape):
        return 1.0 + 0.05 * jax.random.normal(kk, shape, jnp.float32)

    offset = jax.random.randint(k[2], (BATCH, 1), 0, 1024, dtype=jnp.int32)
    return {
        'x': jax.random.normal(k[0], (BATCH, SEQ, D_MODEL), jnp.float32),
        'mem': jax.random.normal(k[1], (BATCH, N_MEM, D_MODEL), jnp.float32),
        'positions': offset + jnp.arange(SEQ, dtype=jnp.int32)[None, :],
        'ffn_norm': gain(k[3], (DEPTH, 2, D_MODEL)),
        'ffn_w_gate': nrm(k[4], (DEPTH, 2, D_MODEL, D_FF), D_MODEL),
        'ffn_w_up': nrm(k[5], (DEPTH, 2, D_MODEL, D_FF), D_MODEL),
        'ffn_w_down': nrm(k[6], (DEPTH, 2, D_FF, D_MODEL), D_FF),
        'mix_norm': gain(k[7], (DEPTH, D_MODEL)),
        'mem_norm': gain(k[8], (DEPTH, D_MODEL)),
        'w_mem_kv': nrm(k[9], (DEPTH, D_MODEL, 2 * MEM_DIM), D_MODEL),
        'w_out': nrm(k[10], (DEPTH, D_MODEL, D_MODEL), D_MODEL),
        'w_in_conv': nrm(k[11], (N_A_LAYERS, D_MODEL, 3 * CONV_CH + MEM_DIM), D_MODEL),
        'conv_w': nrm(k[12], (N_A_LAYERS, CONV_K, CONV_CH), CONV_K),
        'w_in_nsa': nrm(k[13], (N_B_LAYERS, D_MODEL, NSA_HEADS * HEAD_DIM + 3 * NSA_HEADS + MEM_DIM), D_MODEL),
        'kv_norm': gain(k[14], (D_MODEL,)),
        'w_kv': nrm(k[15], (D_MODEL, 6 * KV_WIDTH), D_MODEL),
        'cmp_pos_k': 0.02 * jax.random.normal(k[16], (CMP_BLOCK, HEAD_DIM), jnp.float32),
        'cmp_w1_k': nrm(k[17], (CMP_BLOCK * HEAD_DIM, HEAD_DIM), CMP_BLOCK * HEAD_DIM),
        'cmp_w2_k': nrm(k[18], (HEAD_DIM, HEAD_DIM), HEAD_DIM),
        'cmp_pos_v': 0.02 * jax.random.normal(k[19], (CMP_BLOCK, HEAD_DIM), jnp.float32),
        'cmp_w1_v': nrm(k[20], (CMP_BLOCK * HEAD_DIM, HEAD_DIM), CMP_BLOCK * HEAD_DIM),
        'cmp_w2_v': nrm(k[21], (HEAD_DIM, HEAD_DIM), HEAD_DIM),
        'final_norm': gain(k[22], (D_MODEL,)),
    }


def reference(x, mem, positions, ffn_norm, ffn_w_gate, ffn_w_up, ffn_w_down, mix_norm, mem_norm, w_mem_kv,
              w_out, w_in_conv, conv_w, w_in_nsa, kv_norm, w_kv, cmp_pos_k, cmp_w1_k, cmp_w2_k,
              cmp_pos_v, cmp_w1_v, cmp_w2_v, final_norm):
    b, s, _ = x.shape
    h = x
    kv = None
    for layer in range(DEPTH):
        if layer == N_A_LAYERS:
            kv = shared_nsa_kv(h, positions, kv_norm, w_kv, cmp_pos_k, cmp_w1_k, cmp_w2_k,
                               cmp_pos_v, cmp_w1_v, cmp_w2_v)
        h = h + 0.5 * swiglu(rms_norm(h, ffn_norm[layer, 0]), ffn_w_gate[layer, 0],
                             ffn_w_up[layer, 0], ffn_w_down[layer, 0])
        hn = rms_norm(h, mix_norm[layer])
        if layer < N_A_LAYERS:
            u = hn @ w_in_conv[layer]
            gate_b, gate_c, hv, q_mem = jnp.split(u, [CONV_CH, 2 * CONV_CH, 3 * CONV_CH], axis=-1)
            tok = gate_b * short_conv(gate_c * hv, conv_w[layer])
        else:
            j = layer - N_A_LAYERS
            u = hn @ w_in_nsa[j]
            qd = NSA_HEADS * HEAD_DIM
            q_nsa, gate_logits, q_mem = jnp.split(u, [qd, qd + 3 * NSA_HEADS], axis=-1)
            q = rotary(_heads(q_nsa, NSA_HEADS), positions)
            gates = jax.nn.sigmoid(gate_logits.reshape(b, s, NSA_HEADS, 3)).transpose(0, 2, 1, 3)
            tok = nsa_attention(q, gates, *kv)
        mem_o = memory_attention(q_mem, mem, mem_norm[layer], w_mem_kv[layer])
        h = h + jnp.concatenate([tok, mem_o], axis=-1) @ w_out[layer]
        h = h + 0.5 * swiglu(rms_norm(h, ffn_norm[layer, 1]), ffn_w_gate[layer, 1],
                             ffn_w_up[layer, 1], ffn_w_down[layer, 1])
    return rms_norm(h, final_norm)
```

```python
import functools

import jax
import jax.numpy as jnp
from jax import lax
from jax.experimental import pallas as pl
from jax.experimental.pallas import tpu as pltpu

HEAD_DIM = 128
ROT_DIM = HEAD_DIM // 4
ROPE_THETA = 500000.0
MEM_HEADS = 4
NSA_KV_GROUPS = 4
CMP_STRIDE = 16
CMP_BLOCK = 2 * CMP_STRIDE
SLC_BLOCK = 64
N_SELECT = 16
WINDOW = 512
RMS_EPS = 1e-6
CONV_K = 3

LANES = 128
NEG_BIG = -1e30
VMEM_LIMIT = 56 * 1024 * 1024

F32 = jnp.float32
BF16 = jnp.bfloat16


def _dot(a, b):
    return jnp.dot(a, b, preferred_element_type=F32)


def _dot_nt(a, b):
    return lax.dot_general(a, b, (((1,), (1,)), ((), ())), preferred_element_type=F32)


def _rms(x, g):
    ms = jnp.mean(x * x, axis=-1, keepdims=True)
    return x * lax.rsqrt(ms + RMS_EPS) * g


def _params(sem):
    return pltpu.CompilerParams(dimension_semantics=sem, vmem_limit_bytes=VMEM_LIMIT)


def _rope_table_kernel(pos_ref, inv_ref, cos_ref, sin_ref):
    ang = pos_ref[...].astype(F32) * inv_ref[...]
    lane = lax.broadcasted_iota(jnp.int32, ang.shape, 1)
    c = jnp.cos(ang)
    s = jnp.sin(ang)
    half = ROT_DIM // 2
    cos_ref[...] = jnp.where(lane < ROT_DIM, c, 1.0)
    sin_ref[...] = jnp.where(lane < half, -s, jnp.where(lane < ROT_DIM, s, 0.0))


def _rope_tables(positions_col):
    s = positions_col.shape[0]
    half = ROT_DIM // 2
    inv = 1.0 / (ROPE_THETA ** (jnp.arange(half, dtype=F32) / half))
    inv_row = jnp.concatenate([inv, inv, jnp.zeros((LANES - ROT_DIM,), F32)])[None, :]
    tm = min(s, 1024)
    return pl.pallas_call(
        _rope_table_kernel,
        out_shape=(jax.ShapeDtypeStruct((s, LANES), F32), jax.ShapeDtypeStruct((s, LANES), F32)),
        grid=(s // tm,),
        in_specs=[pl.BlockSpec((tm, 1), lambda i: (i, 0)), pl.BlockSpec((1, LANES), lambda i: (0, 0))],
        out_specs=(pl.BlockSpec((tm, LANES), lambda i: (i, 0)), pl.BlockSpec((tm, LANES), lambda i: (i, 0))),
        compiler_params=_params(("arbitrary",)),
        name="rope_tables",
    )(positions_col, inv_row)


def _rotate(x, cos_t, sin_t):
    half = ROT_DIM // 2
    lane = lax.broadcasted_iota(jnp.int32, x.shape, 1)
    swapped = jnp.where(lane < half, pltpu.roll(x, LANES - half, axis=1), pltpu.roll(x, half, axis=1))
    return x * cos_t + swapped * sin_t


def _ffn_kernel(x_ref, g_ref, wg_ref, wu_ref, wd_ref, *rest, final):
    if final:
        fg_ref, o_ref, xn_ref = rest
    else:
        o_ref, xn_ref = rest
    j = pl.program_id(1)

    @pl.when(j == 0)
    def _():
        x = x_ref[...]
        xn_ref[...] = _rms(x, g_ref[...]).astype(BF16)
        o_ref[...] = x

    xn = xn_ref[...]
    a = _dot(xn, wg_ref[...])
    b = _dot(xn, wu_ref[...])
    mid = (a * jax.nn.sigmoid(a) * b).astype(BF16)
    o_ref[...] += 0.5 * _dot(mid, wd_ref[...])

    if final:
        @pl.when(j == pl.num_programs(1) - 1)
        def _():
            o_ref[...] = _rms(o_ref[...], fg_ref[...])


def _ffn(x, g, wg, wu, wd, final_g=None, *, tm=512, tf=512):
    s, d = x.shape
    f = wg.shape[1]
    final = final_g is not None
    in_specs = [
        pl.BlockSpec((tm, d), lambda i, j: (i, 0)),
        pl.BlockSpec((1, d), lambda i, j: (0, 0)),
        pl.BlockSpec((d, tf), lambda i, j: (0, j)),
        pl.BlockSpec((d, tf), lambda i, j: (0, j)),
        pl.BlockSpec((tf, d), lambda i, j: (j, 0)),
    ]
    args = [x, g[None, :], wg, wu, wd]
    if final:
        in_specs.append(pl.BlockSpec((1, d), lambda i, j: (0, 0)))
        args.append(final_g[None, :])
    return pl.pallas_call(
        functools.partial(_ffn_kernel, final=final),
        out_shape=jax.ShapeDtypeStruct((s, d), F32),
        grid=(s // tm, f // tf),
        in_specs=in_specs,
        out_specs=pl.BlockSpec((tm, d), lambda i, j: (i, 0)),
        scratch_shapes=[pltpu.VMEM((tm, d), BF16)],
        compiler_params=_params(("arbitrary", "arbitrary")),
        name="ffn_final" if final else "ffn",
    )(*args)


def _proj_kernel(x_ref, g_ref, w_ref, *rest, heads, rot_tiles, rot_all, scale, act):
    if rot_tiles:
        cos_ref, sin_ref, o_ref, xn_ref = rest
    else:
        o_ref, xn_ref = rest
    j = pl.program_id(1)

    @pl.when(j == 0)
    def _():
        xn_ref[...] = _rms(x_ref[...], g_ref[...]).astype(BF16)

    acc = _dot(xn_ref[...], w_ref[...])
    if scale != 1.0:
        acc = acc * scale
    if act == "sigmoid":
        acc = jax.nn.sigmoid(acc)

    if not heads:
        o_ref[...] = acc.astype(o_ref.dtype)
        return

    n_h = acc.shape[1] // HEAD_DIM

    def store(rot):
        for h in range(n_h):
            xh = acc[:, h * HEAD_DIM:(h + 1) * HEAD_DIM]
            if rot:
                xh = _rotate(xh, cos_ref[...], sin_ref[...])
            o_ref[h] = xh.astype(o_ref.dtype)

    if rot_tiles == 0:
        store(False)
    elif rot_all:
        store(True)
    else:
        @pl.when(j < rot_tiles)
        def _():
            store(True)

        @pl.when(j >= rot_tiles)
        def _():
            store(False)


def _proj(x, g, w, *, out_dtype, tm=512, tn=512, heads=False, rot_tiles=0, tables=None,
          scale=1.0, act=None):
    s, d = x.shape
    n = w.shape[1]
    tm = min(tm, s)
    tn = min(tn, n)
    in_specs = [
        pl.BlockSpec((tm, d), lambda i, j: (i, 0)),
        pl.BlockSpec((1, d), lambda i, j: (0, 0)),
        pl.BlockSpec((d, tn), lambda i, j: (0, j)),
    ]
    args = [x, g[None, :], w]
    if rot_tiles:
        in_specs += [pl.BlockSpec((tm, LANES), lambda i, j: (i, 0))] * 2
        args += list(tables)
    if heads:
        hp = tn // HEAD_DIM
        out_shape = jax.ShapeDtypeStruct((n // HEAD_DIM, s, HEAD_DIM), out_dtype)
        out_spec = pl.BlockSpec((hp, tm, HEAD_DIM), lambda i, j: (j, i, 0))
    else:
        out_shape = jax.ShapeDtypeStruct((s, n), out_dtype)
        out_spec = pl.BlockSpec((tm, tn), lambda i, j: (i, j))
    return pl.pallas_call(
        functools.partial(_proj_kernel, heads=heads, rot_tiles=rot_tiles, rot_all=rot_tiles >= n // tn,
                          scale=scale, act=act),
        out_shape=out_shape,
        grid=(s // tm, n // tn),
        in_specs=in_specs,
        out_specs=out_spec,
        scratch_shapes=[pltpu.VMEM((tm, d), BF16)],
        compiler_params=_params(("arbitrary", "arbitrary")),
        name="proj",
    )(*args)


def _conv_kernel(gb_ref, gc_ref, hv_ref, pgc_ref, phv_ref, w_ref, o_ref, ext_ref):
    i = pl.program_id(0)
    tm = gb_ref.shape[0]
    halo = pgc_ref.shape[0]
    prev = pgc_ref[...] * phv_ref[...]
    ext_ref[0:halo, :] = jnp.where(i > 0, prev, 0.0)
    ext_ref[halo:, :] = gc_ref[...] * hv_ref[...]
    y = w_ref[CONV_K - 1:CONV_K, :] * ext_ref[halo:, :]
    for k in range(CONV_K - 1):
        shift = CONV_K - 1 - k
        y = y + w_ref[k:k + 1, :] * ext_ref[halo - shift:halo - shift + tm, :]
    o_ref[...] = (gb_ref[...] * y).astype(o_ref.dtype)


def _short_conv(u, conv_w, n_ch, *, tm=512, tc=512, halo=8):
    s = u.shape[0]
    nc = n_ch // tc
    hb = tm // halo
    return pl.pallas_call(
        _conv_kernel,
        out_shape=jax.ShapeDtypeStruct((s, n_ch), BF16),
        grid=(s // tm, nc),
        in_specs=[
            pl.BlockSpec((tm, tc), lambda i, j: (i, j)),
            pl.BlockSpec((tm, tc), lambda i, j: (i, nc + j)),
            pl.BlockSpec((tm, tc), lambda i, j: (i, 2 * nc + j)),
            pl.BlockSpec((halo, tc), lambda i, j: (jnp.maximum(i * hb - 1, 0), nc + j)),
            pl.BlockSpec((halo, tc), lambda i, j: (jnp.maximum(i * hb - 1, 0), 2 * nc + j)),
            pl.BlockSpec((CONV_K, tc), lambda i, j: (0, j)),
        ],
        out_specs=pl.BlockSpec((tm, tc), lambda i, j: (i, j)),
        scratch_shapes=[pltpu.VMEM((tm + halo, tc), F32)],
        compiler_params=_params(("arbitrary", "arbitrary")),
        name="short_conv",
    )(u, u, u, u, u, conv_w)


def _mem_attn_kernel(q_ref, k_ref, v_ref, o_ref):
    scale = HEAD_DIM ** -0.5
    for h in range(MEM_HEADS):
        sl = slice(h * HEAD_DIM, (h + 1) * HEAD_DIM)
        s = _dot_nt(q_ref[:, sl], k_ref[:, sl]) * scale
        m = jnp.max(s, axis=-1, keepdims=True)
        e = jnp.exp(s - m)
        p = e / jnp.sum(e, axis=-1, keepdims=True)
        o_ref[:, sl] = _dot(p.astype(BF16), v_ref[:, sl]).astype(o_ref.dtype)


def _mem_attn(q, kv, *, tm=512):
    s, md = q.shape
    n_mem = kv.shape[0]
    return pl.pallas_call(
        _mem_attn_kernel,
        out_shape=jax.ShapeDtypeStruct((s, md), BF16),
        grid=(s // tm,),
        in_specs=[
            pl.BlockSpec((tm, md), lambda i: (i, 0)),
            pl.BlockSpec((n_mem, md), lambda i: (0, 0)),
            pl.BlockSpec((n_mem, md), lambda i: (0, 1)),
        ],
        out_specs=pl.BlockSpec((tm, md), lambda i: (i, 0)),
        compiler_params=_params(("arbitrary",)),
        name="mem_attn",
    )(q, kv, kv)


def _out_proj_kernel(h_ref, tok_ref, mem_ref, w1_ref, w2_ref, o_ref):
    o_ref[...] = h_ref[...] + _dot(tok_ref[...], w1_ref[...]) + _dot(mem_ref[...], w2_ref[...])


def _out_proj(h, tok, mem_o, w_tok, w_mem, *, tm=512, tn=1024):
    s, d = h.shape
    return pl.pallas_call(
        _out_proj_kernel,
        out_shape=jax.ShapeDtypeStruct((s, d), F32),
        grid=(s // tm, d // tn),
        in_specs=[
            pl.BlockSpec((tm, tn), lambda i, j: (i, j)),
            pl.BlockSpec((tm, tok.shape[1]), lambda i, j: (i, 0)),
            pl.BlockSpec((tm, mem_o.shape[1]), lambda i, j: (i, 0)),
            pl.BlockSpec((w_tok.shape[0], tn), lambda i, j: (0, j)),
            pl.BlockSpec((w_mem.shape[0], tn), lambda i, j: (0, j)),
        ],
        out_specs=pl.BlockSpec((tm, tn), lambda i, j: (i, j)),
        compiler_params=_params(("arbitrary", "arbitrary")),
        name="out_proj",
    )(h, tok, mem_o, w_tok, w_mem)


def _compress_kernel(c_ref, pos_ref, w1_ref, w2_ref, o_ref):
    x = c_ref[0]
    nc, half = x.shape
    w1a = w1_ref[0, 0:half, :]
    w1b = w1_ref[0, half:, :]
    pos = jnp.broadcast_to(pos_ref[0], (8, 2 * half))
    bias = _dot(pos, w1_ref[0])[0:1, :]
    first = _dot(x, w1a)
    second = pltpu.roll(_dot(x, w1b), nc - 1, axis=0)
    hid = jax.nn.gelu(first + second + bias, approximate=True)
    out = _dot(hid.astype(BF16), w2_ref[0])
    row = lax.broadcasted_iota(jnp.int32, out.shape, 0)
    o_ref[0] = jnp.where(row < nc - 1, out, 0.0).astype(o_ref.dtype)


def _compress(chunks, pos, w1, w2):
    n, nc, cw = chunks.shape
    g = n // 2
    return pl.pallas_call(
        _compress_kernel,
        out_shape=jax.ShapeDtypeStruct((n, nc, HEAD_DIM), BF16),
        grid=(n,),
        in_specs=[
            pl.BlockSpec((1, nc, cw), lambda i: (i, 0, 0)),
            pl.BlockSpec((1, 1, 2 * cw), lambda i: (i // g, 0, 0)),
            pl.BlockSpec((1, 2 * cw, HEAD_DIM), lambda i: (i // g, 0, 0)),
            pl.BlockSpec((1, HEAD_DIM, HEAD_DIM), lambda i: (i // g, 0, 0)),
        ],
        out_specs=pl.BlockSpec((1, nc, HEAD_DIM), lambda i: (i, 0, 0)),
        compiler_params=_params(("arbitrary",)),
        name="compress",
    )(chunks, pos, w1, w2)


def _masked_softmax(s, mask):
    s = jnp.where(mask, s, NEG_BIG)
    m = jnp.max(s, axis=-1, keepdims=True)
    e = jnp.where(mask, jnp.exp(s - m), 0.0)
    d = jnp.sum(e, axis=-1, keepdims=True)
    return e / jnp.where(d > 0, d, 1.0)


def _nsa_kernel(q_ref, gate_ref, kc_ref, vc_ref, ov_ref, ks_ref, vs_ref, kw_ref, vw_ref, o_ref,
                m_ref, l_ref, acc_ref, *, tq, tk, rep, n_sel):
    i = pl.program_id(1)
    t0 = i * tq
    rows = rep * tq
    q3 = q_ref[...].reshape(rows, HEAD_DIM)

    def t_of(shape):
        return t0 + (lax.broadcasted_iota(jnp.int32, shape, 0) & (tq - 1))

    kc = kc_ref[0]
    n_cmp = kc.shape[0]
    s_c = _dot_nt(q3, kc)
    cmp_end = lax.broadcasted_iota(jnp.int32, s_c.shape, 1) * CMP_STRIDE + (CMP_BLOCK - 1)
    p_c = _masked_softmax(s_c, cmp_end <= t_of(s_c.shape))
    o_c = _dot(p_c.astype(BF16), vc_ref[0])

    p_sum = p_c[0:tq]
    for r in range(1, rep):
        p_sum = p_sum + p_c[r * tq:(r + 1) * tq]
    p_hi = p_sum.astype(BF16)
    p_lo = (p_sum - p_hi.astype(F32)).astype(BF16)
    imp = _dot(p_hi, ov_ref[...]) + _dot(p_lo, ov_ref[...])
    n_slc = imp.shape[1]
    blk = lax.broadcasted_iota(jnp.int32, imp.shape, 1)
    jt = (t0 + lax.broadcasted_iota(jnp.int32, imp.shape, 0)) // SLC_BLOCK
    valid = blk <= jt
    forced = (blk == 0) | (blk == jt) | (blk == jt - 1)
    work = jnp.where(forced, jnp.inf, jnp.where(valid, imp, -jnp.inf))
    blk_f = blk.astype(F32)
    sel_bias = jnp.full(imp.shape, NEG_BIG, F32)
    for _ in range(n_sel):
        mx = jnp.max(work, axis=-1, keepdims=True)
        first = jnp.min(jnp.where(work == mx, blk_f, float(n_slc)), axis=-1, keepdims=True)
        pick = blk_f == first
        sel_bias = jnp.where(pick, 0.0, sel_bias)
        work = jnp.where(pick, -jnp.inf, work)
    sel_bias = jnp.where(valid, sel_bias, NEG_BIG).astype(BF16)
    q_aug = jnp.concatenate([q3, jnp.concatenate([sel_bias] * rep, axis=0)], axis=1)

    m_ref[...] = jnp.full(m_ref.shape, NEG_BIG, F32)
    l_ref[...] = jnp.zeros(l_ref.shape, F32)
    acc_ref[...] = jnp.zeros(acc_ref.shape, F32)
    t_s = t_of((rows, tk))
    col = lax.broadcasted_iota(jnp.int32, (rows, tk), 1)
    code_row = lax.broadcasted_iota(jnp.int32, (tk, n_slc), 0) // SLC_BLOCK
    code_col = lax.broadcasted_iota(jnp.int32, (tk, n_slc), 1)

    def sel_step(kt, carry):
        k0 = pl.multiple_of(kt * tk, tk)
        code = jnp.where(code_row + kt * (tk // SLC_BLOCK) == code_col, 1.0, 0.0).astype(BF16)
        k_aug = jnp.concatenate([ks_ref[0, pl.ds(k0, tk), :], code], axis=1)
        s = _dot_nt(q_aug, k_aug)
        s = jnp.where(col + k0 <= t_s, s, NEG_BIG)
        m_prev = m_ref[...]
        m_new = jnp.maximum(m_prev, jnp.max(s, axis=-1, keepdims=True))
        alpha = jnp.exp(m_prev - m_new)
        p = jnp.exp(s - jnp.concatenate([m_new] * (tk // LANES), axis=1))
        l_ref[...] = alpha * l_ref[...] + jnp.sum(p, axis=-1, keepdims=True)
        acc_ref[...] = alpha * acc_ref[...] + _dot(p.astype(BF16), vs_ref[0, pl.ds(k0, tk), :])
        m_ref[...] = m_new
        return carry

    lax.fori_loop(0, (t0 + tq + tk - 1) // tk, sel_step, 0)
    o_s = acc_ref[...] / l_ref[...]

    ws = pl.multiple_of(jnp.maximum(t0 - WINDOW, 0), tq)
    wlen = tq + WINDOW
    s_w = _dot_nt(q3, kw_ref[0, pl.ds(ws, wlen), :])
    kt_w = ws + lax.broadcasted_iota(jnp.int32, s_w.shape, 1)
    t_w = t_of(s_w.shape)
    p_w = _masked_softmax(s_w, (kt_w <= t_w) & (kt_w > t_w - WINDOW))
    o_w = _dot(p_w.astype(BF16), vw_ref[0, pl.ds(ws, wlen), :])

    gates = gate_ref[...]
    for r in range(rep):
        rs = slice(r * tq, (r + 1) * tq)
        o = (gates[:, 3 * r:3 * r + 1] * o_c[rs] + gates[:, 3 * r + 1:3 * r + 2] * o_s[rs]
             + gates[:, 3 * r + 2:3 * r + 3] * o_w[rs])
        o_ref[:, r * HEAD_DIM:(r + 1) * HEAD_DIM] = o.astype(o_ref.dtype)


def _nsa(q, gates, cmp_kv, overlap, k_rot, v_all, *, tq=128, tk=512):
    n_h, s, _ = q.shape
    g = NSA_KV_GROUPS
    rep = n_h // g
    nc = cmp_kv.shape[1]
    n_slc = s // SLC_BLOCK
    tk = min(tk, s)
    rows = rep * tq
    kernel = functools.partial(_nsa_kernel, tq=tq, tk=tk, rep=rep, n_sel=min(N_SELECT, n_slc))
    full = lambda off: pl.BlockSpec((1, s, HEAD_DIM), lambda gi, i: (off + gi, 0, 0))
    return pl.pallas_call(
        kernel,
        out_shape=jax.ShapeDtypeStruct((s, n_h * HEAD_DIM), BF16),
        grid=(g, s // tq),
        in_specs=[
            pl.BlockSpec((rep, tq, HEAD_DIM), lambda gi, i: (gi, i, 0)),
            pl.BlockSpec((tq, LANES), lambda gi, i: (i, gi)),
            pl.BlockSpec((1, nc, HEAD_DIM), lambda gi, i: (gi, 0, 0)),
            pl.BlockSpec((1, nc, HEAD_DIM), lambda gi, i: (g + gi, 0, 0)),
            pl.BlockSpec((nc, n_slc), lambda gi, i: (0, 0)),
            full(g), full(g), full(2 * g), full(2 * g),
        ],
        out_specs=pl.BlockSpec((tq, rep * HEAD_DIM), lambda gi, i: (i, gi)),
        scratch_shapes=[pltpu.VMEM((rows, LANES), F32), pltpu.VMEM((rows, LANES), F32),
                        pltpu.VMEM((rows, HEAD_DIM), F32)],
        compiler_params=_params(("arbitrary", "arbitrary")),
        name="nsa",
    )(q, gates, cmp_kv, cmp_kv, overlap, k_rot, v_all, k_rot, v_all)


def _cmp_to_slc(n_cmp, n_slc):
    cs = jnp.arange(n_cmp)[:, None] * CMP_STRIDE
    ss = jnp.arange(n_slc)[None, :] * SLC_BLOCK
    ov = jnp.clip(jnp.minimum(cs + CMP_BLOCK, ss + SLC_BLOCK) - jnp.maximum(cs, ss), 0, None)
    return (ov.astype(F32) / CMP_BLOCK).astype(BF16)


def kernel(x, mem, positions, ffn_norm, ffn_w_gate, ffn_w_up, ffn_w_down, mix_norm, mem_norm, w_mem_kv, w_out, w_in_conv, conv_w, w_in_nsa, kv_norm, w_kv, cmp_pos_k, cmp_w1_k, cmp_w2_k, cmp_pos_v, cmp_w1_v, cmp_w2_v, final_norm):
    b, s, d = x.shape
    assert b == 1
    depth = ffn_norm.shape[0]
    n_a = w_in_conv.shape[0]
    mem_dim = w_mem_kv.shape[2] // 2
    conv_ch = conv_w.shape[2]
    kv_w = w_kv.shape[1] // 6
    g = NSA_KV_GROUPS
    n_q = w_in_nsa.shape[2] - mem_dim
    q_dim = (n_q // (HEAD_DIM + 3)) * HEAD_DIM
    n_heads = q_dim // HEAD_DIM
    rep = n_heads // g
    bf = lambda w: w.astype(BF16)

    h = x[0]
    mem2 = mem[0]
    tables = _rope_tables(positions.reshape(s, 1))

    kv_side = None
    for layer in range(depth):
        if layer == n_a:
            wk = w_kv.reshape(d, 3, 2, kv_w)
            w_re = jnp.concatenate([wk[:, :, 0, :].reshape(d, 3 * kv_w), wk[:, :, 1, :].reshape(d, 3 * kv_w)], axis=1)
            kvh = _proj(h, kv_norm, bf(w_re), out_dtype=BF16, tn=kv_w, heads=True, rot_tiles=3, tables=tables)
            k_rot, v_all = kvh[:3 * g], kvh[3 * g:]
            nc = s // CMP_STRIDE
            chunks = jnp.concatenate([k_rot[:g], v_all[:g]], axis=0).reshape(2 * g, nc, CMP_STRIDE * HEAD_DIM)
            pos = bf(jnp.stack([cmp_pos_k, cmp_pos_v]).reshape(2, 1, CMP_BLOCK * HEAD_DIM))
            cmp_kv = _compress(chunks, pos, bf(jnp.stack([cmp_w1_k, cmp_w1_v])), bf(jnp.stack([cmp_w2_k, cmp_w2_v])))
            kv_side = (cmp_kv, k_rot, v_all, _cmp_to_slc(nc, s // SLC_BLOCK))

        h = _ffn(h, ffn_norm[layer, 0], bf(ffn_w_gate[layer, 0]), bf(ffn_w_up[layer, 0]), bf(ffn_w_down[layer, 0]))

        if layer < n_a:
            w_in = w_in_conv[layer]
            u = _proj(h, mix_norm[layer], bf(w_in[:, :3 * conv_ch]), out_dtype=F32)
            tok = _short_conv(u, conv_w[layer], conv_ch)
            q_mem = _proj(h, mix_norm[layer], bf(w_in[:, 3 * conv_ch:]), out_dtype=BF16)
        else:
            w_in = w_in_nsa[layer - n_a]
            q = _proj(h, mix_norm[layer], bf(w_in[:, :q_dim]), out_dtype=BF16, heads=True,
                      rot_tiles=q_dim // 512, tables=tables, scale=HEAD_DIM ** -0.5)
            wg = w_in[:, q_dim:q_dim + 3 * n_heads].reshape(d, g, 3 * rep)
            wg = jnp.pad(wg, ((0, 0), (0, 0), (0, LANES - 3 * rep))).reshape(d, g * LANES)
            gates = _proj(h, mix_norm[layer], bf(wg), out_dtype=F32, act="sigmoid")
            q_mem = _proj(h, mix_norm[layer], bf(w_in[:, q_dim + 3 * n_heads:]), out_dtype=BF16)
            cmp_kv, k_rot, v_all, overlap = kv_side
            tok = _nsa(q, gates, cmp_kv, overlap, k_rot, v_all)

        mem_kv = _proj(mem2, mem_norm[layer], bf(w_mem_kv[layer]), out_dtype=BF16)
        mem_o = _mem_attn(q_mem, mem_kv)
        n_tok = tok.shape[1]
        h = _out_proj(h, tok, mem_o, bf(w_out[layer, :n_tok]), bf(w_out[layer, n_tok:]))

        last = layer == depth - 1
        h = _ffn(h, ffn_norm[layer, 1], bf(ffn_w_gate[layer, 1]), bf(ffn_w_up[layer, 1]), bf(ffn_w_down[layer, 1]),
                 final_norm if last else None)

    return h[None]
```

```python
import functools
import math

import jax
import jax.numpy as jnp
from jax import lax
from jax.experimental import pallas as pl
from jax.experimental.pallas import tpu as pltpu

HEAD_DIM = 128
ROT_DIM = HEAD_DIM // 4
ROPE_THETA = 500000.0
MEM_HEADS = 4
NSA_KV_GROUPS = 4
CMP_STRIDE = 16
CMP_BLOCK = 2 * CMP_STRIDE
SLC_BLOCK = 64
N_SELECT = 16
N_FORCED = 3
WINDOW = 512
RMS_EPS = 1e-6
CONV_K = 3

LANES = 128
SUBLANES = 8
NEG_BIG = -1e30
VMEM_LIMIT = 56 * 1024 * 1024

F32 = jnp.float32
BF16 = jnp.bfloat16


def _dot(a, b):
    return jnp.dot(a, b, preferred_element_type=F32)


def _dot_nt(a, b):
    return lax.dot_general(a, b, (((1,), (1,)), ((), ())), preferred_element_type=F32)


def _rms(x, g):
    ms = jnp.mean(x * x, axis=-1, keepdims=True)
    return x * lax.rsqrt(ms + RMS_EPS) * g


def _params(sem):
    return pltpu.CompilerParams(dimension_semantics=sem, vmem_limit_bytes=VMEM_LIMIT)


def _rope_table_kernel(pos_ref, inv_ref, cos_ref, sin_ref):
    ang = pos_ref[...].astype(F32) * inv_ref[...]
    lane = lax.broadcasted_iota(jnp.int32, ang.shape, 1)
    c = jnp.cos(ang)
    s = jnp.sin(ang)
    half = ROT_DIM // 2
    cos_ref[...] = jnp.where(lane < ROT_DIM, c, 1.0)
    sin_ref[...] = jnp.where(lane < half, -s, jnp.where(lane < ROT_DIM, s, 0.0))


def _rope_tables(positions_col):
    s = positions_col.shape[0]
    half = ROT_DIM // 2
    inv = 1.0 / (ROPE_THETA ** (jnp.arange(half, dtype=F32) / half))
    inv_row = jnp.concatenate([inv, inv, jnp.zeros((LANES - ROT_DIM,), F32)])[None, :]
    tm = min(s, 1024)
    return pl.pallas_call(
        _rope_table_kernel,
        out_shape=(jax.ShapeDtypeStruct((s, LANES), F32), jax.ShapeDtypeStruct((s, LANES), F32)),
        grid=(s // tm,),
        in_specs=[pl.BlockSpec((tm, 1), lambda i: (i, 0)), pl.BlockSpec((1, LANES), lambda i: (0, 0))],
        out_specs=(pl.BlockSpec((tm, LANES), lambda i: (i, 0)), pl.BlockSpec((tm, LANES), lambda i: (i, 0))),
        compiler_params=_params(("arbitrary",)),
        name="rope_tables",
    )(positions_col, inv_row)


def _rotate(x, cos_t, sin_t):
    half = ROT_DIM // 2
    lane = lax.broadcasted_iota(jnp.int32, x.shape, 1)
    swapped = jnp.where(lane < half, pltpu.roll(x, LANES - half, axis=1), pltpu.roll(x, half, axis=1))
    return x * cos_t + swapped * sin_t


def _store_heads(o_ref, acc, tables):
    for h in range(acc.shape[1] // HEAD_DIM):
        xh = acc[:, h * HEAD_DIM:(h + 1) * HEAD_DIM]
        if tables is not None:
            xh = _rotate(xh, tables[0][...], tables[1][...])
        o_ref[h] = xh.astype(o_ref.dtype)


def _ffn_kernel(x_ref, g_ref, wg_ref, wu_ref, wd_ref, *rest, final):
    if final:
        fg_ref, o_ref, xn_ref = rest
    else:
        o_ref, xn_ref = rest
    j = pl.program_id(1)

    @pl.when(j == 0)
    def _():
        x = x_ref[...]
        xn_ref[...] = _rms(x, g_ref[...]).astype(BF16)
        o_ref[...] = x

    xn = xn_ref[...]
    a = _dot(xn, wg_ref[...])
    b = _dot(xn, wu_ref[...])
    mid = (a * jax.nn.sigmoid(a) * b).astype(BF16)
    o_ref[...] += 0.5 * _dot(mid, wd_ref[...])

    if final:
        @pl.when(j == pl.num_programs(1) - 1)
        def _():
            o_ref[...] = _rms(o_ref[...], fg_ref[...])


def _ffn(x, g, wg, wu, wd, final_g=None, *, tm=512, tf=512):
    s, d = x.shape
    f = wg.shape[1]
    final = final_g is not None
    in_specs = [
        pl.BlockSpec((tm, d), lambda i, j: (i, 0)),
        pl.BlockSpec((1, d), lambda i, j: (0, 0)),
        pl.BlockSpec((d, tf), lambda i, j: (0, j)),
        pl.BlockSpec((d, tf), lambda i, j: (0, j)),
        pl.BlockSpec((tf, d), lambda i, j: (j, 0)),
    ]
    args = [x, g[None, :], wg, wu, wd]
    if final:
        in_specs.append(pl.BlockSpec((1, d), lambda i, j: (0, 0)))
        args.append(final_g[None, :])
    return pl.pallas_call(
        functools.partial(_ffn_kernel, final=final),
        out_shape=jax.ShapeDtypeStruct((s, d), F32),
        grid=(s // tm, f // tf),
        in_specs=in_specs,
        out_specs=pl.BlockSpec((tm, d), lambda i, j: (i, 0)),
        scratch_shapes=[pltpu.VMEM((tm, d), BF16)],
        compiler_params=_params(("arbitrary", "arbitrary")),
        name="ffn_final" if final else "ffn",
    )(*args)


def _proj_kernel(x_ref, g_ref, w_ref, *rest, heads, rot_tiles):
    if rot_tiles:
        cos_ref, sin_ref, o_ref, xn_ref = rest
    else:
        o_ref, xn_ref = rest
    j = pl.program_id(1)

    @pl.when(j == 0)
    def _():
        xn_ref[...] = _rms(x_ref[...], g_ref[...]).astype(BF16)

    acc = _dot(xn_ref[...], w_ref[...])
    if not heads:
        o_ref[...] = acc.astype(o_ref.dtype)
        return

    @pl.when(j < rot_tiles)
    def _():
        _store_heads(o_ref, acc, (cos_ref, sin_ref))

    @pl.when(j >= rot_tiles)
    def _():
        _store_heads(o_ref, acc, None)


def _proj(x, g, w, *, tm=1024, tn=512, heads=False, rot_tiles=0, tables=None):
    s, d = x.shape
    n = w.shape[1]
    tm = min(tm, s)
    tn = min(tn, n)
    in_specs = [
        pl.BlockSpec((tm, d), lambda i, j: (i, 0)),
        pl.BlockSpec((1, d), lambda i, j: (0, 0)),
        pl.BlockSpec((d, tn), lambda i, j: (0, j)),
    ]
    args = [x, g[None, :], w]
    if rot_tiles:
        in_specs += [pl.BlockSpec((tm, LANES), lambda i, j: (i, 0))] * 2
        args += list(tables)
    if heads:
        out_shape = jax.ShapeDtypeStruct((n // HEAD_DIM, s, HEAD_DIM), BF16)
        out_spec = pl.BlockSpec((tn // HEAD_DIM, tm, HEAD_DIM), lambda i, j: (j, i, 0))
    else:
        out_shape = jax.ShapeDtypeStruct((s, n), BF16)
        out_spec = pl.BlockSpec((tm, tn), lambda i, j: (i, j))
    return pl.pallas_call(
        functools.partial(_proj_kernel, heads=heads, rot_tiles=rot_tiles),
        out_shape=out_shape,
        grid=(s // tm, n // tn),
        in_specs=in_specs,
        out_specs=out_spec,
        scratch_shapes=[pltpu.VMEM((tm, d), BF16)],
        compiler_params=_params(("arbitrary", "arbitrary")),
        name="proj",
    )(*args)


def _mix_a_kernel(x_ref, g_ref, wc_ref, wq_ref, cw_ref, tok_ref, qmem_ref, xn_ref, halo_ref, ext_ref, *, tc):
    i = pl.program_id(0)
    j = pl.program_id(1)
    tm = x_ref.shape[0]

    @pl.when(j == 0)
    def _():
        xn_ref[...] = _rms(x_ref[...], g_ref[...]).astype(BF16)
        qmem_ref[...] = _dot(xn_ref[...], wq_ref[...]).astype(qmem_ref.dtype)

    u = _dot(xn_ref[...], wc_ref[0])
    gate_b = u[:, :tc]
    p = u[:, tc:2 * tc] * u[:, 2 * tc:]

    @pl.when(i == 0)
    def _():
        halo_ref[j] = jnp.zeros((SUBLANES, tc), F32)

    ext_ref[0:SUBLANES, :] = halo_ref[j]
    ext_ref[SUBLANES:, :] = p
    halo_ref[j] = p[tm - SUBLANES:, :]
    y = cw_ref[CONV_K - 1:CONV_K, :] * p
    for k in range(CONV_K - 1):
        off = SUBLANES - (CONV_K - 1 - k)
        y = y + cw_ref[k:k + 1, :] * ext_ref[off:off + tm, :]
    tok_ref[...] = (gate_b * y).astype(tok_ref.dtype)


def _mix_a(x, g, w_conv, w_qmem, conv_w, *, tm=1024, tc=256):
    s, d = x.shape
    n_ch = conv_w.shape[1]
    nt = n_ch // tc
    md = w_qmem.shape[1]
    wc = w_conv.reshape(d, 3, nt, tc).transpose(2, 0, 1, 3).reshape(nt, d, 3 * tc)
    return pl.pallas_call(
        functools.partial(_mix_a_kernel, tc=tc),
        out_shape=(jax.ShapeDtypeStruct((s, n_ch), BF16), jax.ShapeDtypeStruct((s, md), BF16)),
        grid=(s // tm, nt),
        in_specs=[
            pl.BlockSpec((tm, d), lambda i, j: (i, 0)),
            pl.BlockSpec((1, d), lambda i, j: (0, 0)),
            pl.BlockSpec((1, d, 3 * tc), lambda i, j: (j, 0, 0)),
            pl.BlockSpec((d, md), lambda i, j: (0, 0)),
            pl.BlockSpec((CONV_K, tc), lambda i, j: (0, j)),
        ],
        out_specs=(pl.BlockSpec((tm, tc), lambda i, j: (i, j)), pl.BlockSpec((tm, md), lambda i, j: (i, 0))),
        scratch_shapes=[pltpu.VMEM((tm, d), BF16), pltpu.VMEM((nt, SUBLANES, tc), F32),
                        pltpu.VMEM((tm + SUBLANES, tc), F32)],
        compiler_params=_params(("arbitrary", "arbitrary")),
        name="mix_a_in",
    )(x, g[None, :], wc, w_qmem, conv_w)


def _mix_b_kernel(x_ref, g_ref, w_ref, cos_ref, sin_ref, q_ref, qmem_ref, gate_ref, xn_ref, *, q_tiles, m_tiles, scale):
    j = pl.program_id(1)

    @pl.when(j == 0)
    def _():
        xn_ref[...] = _rms(x_ref[...], g_ref[...]).astype(BF16)

    acc = _dot(xn_ref[...], w_ref[...])

    @pl.when(j < q_tiles)
    def _():
        _store_heads(q_ref, acc * scale, (cos_ref, sin_ref))

    @pl.when((j >= q_tiles) & (j < q_tiles + m_tiles))
    def _():
        qmem_ref[...] = acc.astype(qmem_ref.dtype)

    @pl.when(j >= q_tiles + m_tiles)
    def _():
        gate_ref[...] = jax.nn.sigmoid(acc)


def _mix_b(x, g, w, tables, *, q_dim, mem_dim, scale, tm=1024, tn=512):
    s, d = x.shape
    n = w.shape[1]
    q_tiles = q_dim // tn
    m_tiles = mem_dim // tn
    hp = tn // HEAD_DIM
    return pl.pallas_call(
        functools.partial(_mix_b_kernel, q_tiles=q_tiles, m_tiles=m_tiles, scale=scale),
        out_shape=(jax.ShapeDtypeStruct((q_dim // HEAD_DIM, s, HEAD_DIM), BF16),
                   jax.ShapeDtypeStruct((s, mem_dim), BF16),
                   jax.ShapeDtypeStruct((s, n - q_dim - mem_dim), F32)),
        grid=(s // tm, n // tn),
        in_specs=[
            pl.BlockSpec((tm, d), lambda i, j: (i, 0)),
            pl.BlockSpec((1, d), lambda i, j: (0, 0)),
            pl.BlockSpec((d, tn), lambda i, j: (0, j)),
            pl.BlockSpec((tm, LANES), lambda i, j: (i, 0)),
            pl.BlockSpec((tm, LANES), lambda i, j: (i, 0)),
        ],
        out_specs=(
            pl.BlockSpec((hp, tm, HEAD_DIM), lambda i, j: (jnp.minimum(j, q_tiles - 1), i, 0)),
            pl.BlockSpec((tm, tn), lambda i, j: (i, jnp.clip(j - q_tiles, 0, m_tiles - 1))),
            pl.BlockSpec((tm, tn), lambda i, j: (i, jnp.maximum(j - q_tiles - m_tiles, 0))),
        ),
        scratch_shapes=[pltpu.VMEM((tm, d), BF16)],
        compiler_params=_params(("arbitrary", "arbitrary")),
        name="mix_b_in",
    )(x, g[None, :], w, *tables)


def _mem_attn_kernel(q_ref, k_ref, v_ref, o_ref):
    scale = HEAD_DIM ** -0.5
    for h in range(MEM_HEADS):
        sl = slice(h * HEAD_DIM, (h + 1) * HEAD_DIM)
        s = _dot_nt(q_ref[:, sl], k_ref[:, sl]) * scale
        m = jnp.max(s, axis=-1, keepdims=True)
        e = jnp.exp(s - m)
        p = e / jnp.sum(e, axis=-1, keepdims=True)
        o_ref[:, sl] = _dot(p.astype(BF16), v_ref[:, sl]).astype(o_ref.dtype)


def _mem_attn(q, kv, *, tm=512):
    s, md = q.shape
    n_mem = kv.shape[0]
    return pl.pallas_call(
        _mem_attn_kernel,
        out_shape=jax.ShapeDtypeStruct((s, md), BF16),
        grid=(s // tm,),
        in_specs=[
            pl.BlockSpec((tm, md), lambda i: (i, 0)),
            pl.BlockSpec((n_mem, md), lambda i: (0, 0)),
            pl.BlockSpec((n_mem, md), lambda i: (0, 1)),
        ],
        out_specs=pl.BlockSpec((tm, md), lambda i: (i, 0)),
        compiler_params=_params(("arbitrary",)),
        name="mem_attn",
    )(q, kv, kv)


def _out_proj_kernel(h_ref, tok_ref, mem_ref, w1_ref, w2_ref, o_ref):
    o_ref[...] = h_ref[...] + _dot(tok_ref[...], w1_ref[...]) + _dot(mem_ref[...], w2_ref[...])


def _out_proj(h, tok, mem_o, w_tok, w_mem, *, tm=512, tn=1024):
    s, d = h.shape
    return pl.pallas_call(
        _out_proj_kernel,
        out_shape=jax.ShapeDtypeStruct((s, d), F32),
        grid=(s // tm, d // tn),
        in_specs=[
            pl.BlockSpec((tm, tn), lambda i, j: (i, j)),
            pl.BlockSpec((tm, tok.shape[1]), lambda i, j: (i, 0)),
            pl.BlockSpec((tm, mem_o.shape[1]), lambda i, j: (i, 0)),
            pl.BlockSpec((w_tok.shape[0], tn), lambda i, j: (0, j)),
            pl.BlockSpec((w_mem.shape[0], tn), lambda i, j: (0, j)),
        ],
        out_specs=pl.BlockSpec((tm, tn), lambda i, j: (i, j)),
        compiler_params=_params(("arbitrary", "arbitrary")),
        name="out_proj",
    )(h, tok, mem_o, w_tok, w_mem)


def _compress_kernel(c_ref, pos_ref, w1_ref, w2_ref, o_ref):
    x = c_ref[0]
    nc, half = x.shape
    w1a = w1_ref[0, 0:half, :]
    w1b = w1_ref[0, half:, :]
    pos = jnp.broadcast_to(pos_ref[0], (SUBLANES, 2 * half))
    bias = _dot(pos, w1_ref[0])[0:1, :]
    first = _dot(x, w1a)
    second = pltpu.roll(_dot(x, w1b), nc - 1, axis=0)
    hid = jax.nn.gelu(first + second + bias, approximate=True)
    out = _dot(hid.astype(BF16), w2_ref[0])
    row = lax.broadcasted_iota(jnp.int32, out.shape, 0)
    o_ref[0] = jnp.where(row < nc - 1, out, 0.0).astype(o_ref.dtype)


def _compress(kvh_chunks, pos, w1, w2, *, g):
    _, nc, cw = kvh_chunks.shape
    return pl.pallas_call(
        _compress_kernel,
        out_shape=jax.ShapeDtypeStruct((2 * g, nc, HEAD_DIM), BF16),
        grid=(2 * g,),
        in_specs=[
            pl.BlockSpec((1, nc, cw), lambda i: (i + (i // g) * 2 * g, 0, 0)),
            pl.BlockSpec((1, 1, 2 * cw), lambda i: (i // g, 0, 0)),
            pl.BlockSpec((1, 2 * cw, HEAD_DIM), lambda i: (i // g, 0, 0)),
            pl.BlockSpec((1, HEAD_DIM, HEAD_DIM), lambda i: (i // g, 0, 0)),
        ],
        out_specs=pl.BlockSpec((1, nc, HEAD_DIM), lambda i: (i, 0, 0)),
        compiler_params=_params(("arbitrary",)),
        name="compress",
    )(kvh_chunks, pos, w1, w2)


def _masked_softmax2(s, mask):
    s = jnp.where(mask, s, NEG_BIG)
    m = jnp.max(s, axis=-1, keepdims=True)
    e = jnp.where(mask, jnp.exp2(s - m), 0.0)
    d = jnp.sum(e, axis=-1, keepdims=True)
    return e / jnp.where(d > 0, d, 1.0)


def _nsa_cmp_kernel(q_ref, gate_ref, kc_ref, vc_ref, ov_ref, kw_ref, vw_ref, ocw_ref, bias_ref,
                    *, tq, tw, rep, n_pick):
    i = pl.program_id(1)
    t0 = i * tq
    rows = rep * tq
    q3 = q_ref[...].reshape(rows, HEAD_DIM)
    gates = gate_ref[...]

    kc = kc_ref[0]
    s_c = _dot_nt(q3, kc)
    cmp_end = lax.broadcasted_iota(jnp.int32, s_c.shape, 1) * CMP_STRIDE + (CMP_BLOCK - 1)
    t_c = t0 + (lax.broadcasted_iota(jnp.int32, s_c.shape, 0) & (tq - 1))
    p_c = _masked_softmax2(s_c, cmp_end <= t_c)
    o_c = _dot(p_c.astype(BF16), vc_ref[0])

    p_sum = p_c[0:tq]
    for r in range(1, rep):
        p_sum = p_sum + p_c[r * tq:(r + 1) * tq]
    p_hi = p_sum.astype(BF16)
    p_lo = (p_sum - p_hi.astype(F32)).astype(BF16)
    imp = _dot(p_hi, ov_ref[...]) + _dot(p_lo, ov_ref[...])
    n_slc = imp.shape[1]
    blk = lax.broadcasted_iota(jnp.int32, imp.shape, 1)
    jt = (t0 + lax.broadcasted_iota(jnp.int32, imp.shape, 0)) // SLC_BLOCK
    valid = blk <= jt
    forced = (blk == 0) | (blk == jt) | (blk == jt - 1)
    work = jnp.where(valid, jnp.where(forced, -jnp.inf, imp), -jnp.inf)
    blk_f = blk.astype(F32)
    sel_bias = jnp.where(forced, 0.0, NEG_BIG)
    for _ in range(n_pick):
        mx = jnp.max(work, axis=-1, keepdims=True)
        first = jnp.min(jnp.where(work == mx, blk_f, float(n_slc)), axis=-1, keepdims=True)
        pick = blk_f == first
        sel_bias = jnp.where(pick, 0.0, sel_bias)
        work = jnp.where(pick, -jnp.inf, work)
    bias_ref[0] = jnp.where(valid, sel_bias, NEG_BIG).astype(bias_ref.dtype)

    wlen = tw + WINDOW
    for sub in range(tq // tw):
        ts0 = t0 + sub * tw
        ws = pl.multiple_of(jnp.maximum(ts0 - WINDOW, 0), tw)
        qs = jnp.concatenate([q_ref[r, sub * tw:(sub + 1) * tw, :] for r in range(rep)], axis=0)
        s_w = _dot_nt(qs, kw_ref[0, pl.ds(ws, wlen), :])
        kt_w = ws + lax.broadcasted_iota(jnp.int32, s_w.shape, 1)
        t_w = ts0 + (lax.broadcasted_iota(jnp.int32, s_w.shape, 0) & (tw - 1))
        p_w = _masked_softmax2(s_w, (kt_w <= t_w) & (kt_w > t_w - WINDOW))
        o_w = _dot(p_w.astype(BF16), vw_ref[0, pl.ds(ws, wlen), :])
        gs = gates[sub * tw:(sub + 1) * tw]
        for r in range(rep):
            oc_r = o_c[r * tq + sub * tw:r * tq + (sub + 1) * tw]
            ow_r = o_w[r * tw:(r + 1) * tw]
            ocw_ref[sub * tw:(sub + 1) * tw, r * HEAD_DIM:(r + 1) * HEAD_DIM] = (
                gs[:, 3 * r:3 * r + 1] * oc_r + gs[:, 3 * r + 2:3 * r + 3] * ow_r)


def _nsa_cmp(q, gates, cmp_kv, overlap, kvh, *, tq=512, tw=128):
    n_h, s, _ = q.shape
    g = NSA_KV_GROUPS
    rep = n_h // g
    nc = cmp_kv.shape[1]
    n_slc = s // SLC_BLOCK
    tq = min(tq, s)
    n_pick = max(min(N_SELECT, n_slc) - N_FORCED, 0)
    kernel = functools.partial(_nsa_cmp_kernel, tq=tq, tw=tw, rep=rep, n_pick=n_pick)
    return pl.pallas_call(
        kernel,
        out_shape=(jax.ShapeDtypeStruct((s, n_h * HEAD_DIM), F32), jax.ShapeDtypeStruct((g, s, n_slc), BF16)),
        grid=(g, s // tq),
        in_specs=[
            pl.BlockSpec((rep, tq, HEAD_DIM), lambda gi, i: (gi, i, 0)),
            pl.BlockSpec((tq, LANES), lambda gi, i: (i, gi)),
            pl.BlockSpec((1, nc, HEAD_DIM), lambda gi, i: (gi, 0, 0)),
            pl.BlockSpec((1, nc, HEAD_DIM), lambda gi, i: (g + gi, 0, 0)),
            pl.BlockSpec((nc, n_slc), lambda gi, i: (0, 0)),
            pl.BlockSpec((1, s, HEAD_DIM), lambda gi, i: (2 * g + gi, 0, 0)),
            pl.BlockSpec((1, s, HEAD_DIM), lambda gi, i: (5 * g + gi, 0, 0)),
        ],
        out_specs=(pl.BlockSpec((tq, rep * HEAD_DIM), lambda gi, i: (i, gi)),
                   pl.BlockSpec((1, tq, n_slc), lambda gi, i: (gi, i, 0))),
        compiler_params=_params(("arbitrary", "arbitrary")),
        name="nsa_cmp_win",
    )(q, gates, cmp_kv, cmp_kv, overlap, kvh, kvh)


def _nsa_sel_kernel(q_ref, bias_ref, ks_ref, vs_ref, ocw_ref, gate_ref, o_ref, m_ref, acc_ref, *, tq, tk, rep):
    i = pl.program_id(1)
    t0 = i * tq
    rows = rep * tq
    q3 = q_ref[...].reshape(rows, HEAD_DIM)
    q_aug = jnp.concatenate([q3, jnp.concatenate([bias_ref[0]] * rep, axis=0)], axis=1)

    m_ref[...] = jnp.full(m_ref.shape, NEG_BIG, F32)
    acc_ref[...] = jnp.zeros(acc_ref.shape, F32)

    def step(kt, causal):
        k0 = pl.multiple_of(kt * tk, tk)
        k_aug = ks_ref[0, pl.ds(k0, tk), :]
        v_aug = vs_ref[0, pl.ds(k0, tk), :]
        for r in range(rep):
            rs = slice(r * tq, (r + 1) * tq)
            s = _dot_nt(q_aug[rs], k_aug)
            if causal:
                t_s = t0 + lax.broadcasted_iota(jnp.int32, s.shape, 0)
                s = jnp.where(lax.broadcasted_iota(jnp.int32, s.shape, 1) + k0 <= t_s, s, NEG_BIG)
            m_prev = m_ref[rs, :]
            m_new = jnp.maximum(m_prev, jnp.max(s, axis=-1, keepdims=True))
            alpha = jnp.exp2(m_prev - m_new)
            p = jnp.exp2(s - jnp.concatenate([m_new] * (tk // LANES), axis=1))
            acc_ref[rs, :] = (jnp.concatenate([alpha, alpha], axis=1) * acc_ref[rs, :]
                              + _dot(p.astype(BF16), v_aug))
            m_ref[rs, :] = m_new

    n_tiles = (t0 + tq + tk - 1) // tk

    def body(kt, carry):
        step(kt, False)
        return carry

    lax.fori_loop(0, n_tiles - 1, body, 0)
    step(n_tiles - 1, True)

    acc = acc_ref[...]
    o_s = acc[:, :HEAD_DIM] / acc[:, HEAD_DIM:]
    gates = gate_ref[...]
    for r in range(rep):
        cs = slice(r * HEAD_DIM, (r + 1) * HEAD_DIM)
        o_ref[:, cs] = (ocw_ref[:, cs] + gates[:, 3 * r + 1:3 * r + 2] * o_s[r * tq:(r + 1) * tq]).astype(o_ref.dtype)


def _nsa_sel(q, gates, sel_bias, ks_aug, vs_aug, ocw, *, tq=512, tk=512):
    n_h, s, _ = q.shape
    g = NSA_KV_GROUPS
    rep = n_h // g
    n_slc = sel_bias.shape[2]
    tk = min(tk, s)
    tq = min(tq, tk)
    rows = rep * tq
    kernel = functools.partial(_nsa_sel_kernel, tq=tq, tk=tk, rep=rep)
    return pl.pallas_call(
        kernel,
        out_shape=jax.ShapeDtypeStruct((s, n_h * HEAD_DIM), BF16),
        grid=(g, s // tq),
        in_specs=[
            pl.BlockSpec((rep, tq, HEAD_DIM), lambda gi, i: (gi, i, 0)),
            pl.BlockSpec((1, tq, n_slc), lambda gi, i: (gi, i, 0)),
            pl.BlockSpec((1, s, HEAD_DIM + n_slc), lambda gi, i: (gi, 0, 0)),
            pl.BlockSpec((1, s, 2 * HEAD_DIM), lambda gi, i: (gi, 0, 0)),
            pl.BlockSpec((tq, rep * HEAD_DIM), lambda gi, i: (i, gi)),
            pl.BlockSpec((tq, LANES), lambda gi, i: (i, gi)),
        ],
        out_specs=pl.BlockSpec((tq, rep * HEAD_DIM), lambda gi, i: (i, gi)),
        scratch_shapes=[pltpu.VMEM((rows, LANES), F32), pltpu.VMEM((rows, 2 * HEAD_DIM), F32)],
        compiler_params=_params(("arbitrary", "arbitrary")),
        name="nsa_sel",
    )(q, sel_bias, ks_aug, vs_aug, ocw, gates)


def _cmp_to_slc(n_cmp, n_slc):
    cs = jnp.arange(n_cmp)[:, None] * CMP_STRIDE
    ss = jnp.arange(n_slc)[None, :] * SLC_BLOCK
    ov = jnp.clip(jnp.minimum(cs + CMP_BLOCK, ss + SLC_BLOCK) - jnp.maximum(cs, ss), 0, None)
    return (ov.astype(F32) / CMP_BLOCK).astype(BF16)


def kernel(x, mem, positions, ffn_norm, ffn_w_gate, ffn_w_up, ffn_w_down, mix_norm, mem_norm, w_mem_kv, w_out, w_in_conv, conv_w, w_in_nsa, kv_norm, w_kv, cmp_pos_k, cmp_w1_k, cmp_w2_k, cmp_pos_v, cmp_w1_v, cmp_w2_v, final_norm):
    b, s, d = x.shape
    assert b == 1
    depth = ffn_norm.shape[0]
    n_a = w_in_conv.shape[0]
    mem_dim = w_mem_kv.shape[2] // 2
    conv_ch = conv_w.shape[2]
    kv_w = w_kv.shape[1] // 6
    g = NSA_KV_GROUPS
    n_q = w_in_nsa.shape[2] - mem_dim
    n_heads = n_q // (HEAD_DIM + 3)
    q_dim = n_heads * HEAD_DIM
    rep = n_heads // g
    n_slc = s // SLC_BLOCK
    bf = lambda w: w.astype(BF16)

    h = x[0]
    mem2 = mem[0]
    tables = _rope_tables(positions.reshape(s, 1))

    kv_side = None
    for layer in range(depth):
        if layer == n_a:
            wk = w_kv.reshape(d, 3, 2, kv_w)
            w_re = jnp.concatenate([wk[:, :, 0, :].reshape(d, 3 * kv_w), wk[:, :, 1, :].reshape(d, 3 * kv_w)], axis=1)
            kvh = _proj(h, kv_norm, bf(w_re), tn=kv_w, heads=True, rot_tiles=3, tables=tables)
            nc = s // CMP_STRIDE
            pos = bf(jnp.stack([cmp_pos_k, cmp_pos_v]).reshape(2, 1, CMP_BLOCK * HEAD_DIM))
            cmp_kv = _compress(kvh.reshape(6 * g, nc, CMP_STRIDE * HEAD_DIM), pos,
                               bf(jnp.stack([cmp_w1_k, cmp_w1_v])), bf(jnp.stack([cmp_w2_k, cmp_w2_v])), g=g)
            code = (jnp.arange(s)[:, None] // SLC_BLOCK == jnp.arange(n_slc)[None, :]).astype(BF16)
            ks_aug = jnp.concatenate([kvh[g:2 * g], jnp.broadcast_to(code, (g, s, n_slc))], axis=-1)
            vs_aug = jnp.concatenate([kvh[4 * g:5 * g], jnp.ones((g, s, HEAD_DIM), BF16)], axis=-1)
            kv_side = (cmp_kv, kvh, ks_aug, vs_aug, _cmp_to_slc(nc, n_slc))

        h = _ffn(h, ffn_norm[layer, 0], bf(ffn_w_gate[layer, 0]), bf(ffn_w_up[layer, 0]), bf(ffn_w_down[layer, 0]))

        if layer < n_a:
            w_in = w_in_conv[layer]
            tok, q_mem = _mix_a(h, mix_norm[layer], bf(w_in[:, :3 * conv_ch]), bf(w_in[:, 3 * conv_ch:]), conv_w[layer])
        else:
            w_in = w_in_nsa[layer - n_a]
            wg = w_in[:, q_dim:q_dim + 3 * n_heads].reshape(d, g, 3 * rep)
            wg = jnp.pad(wg, ((0, 0), (0, 0), (0, LANES - 3 * rep))).reshape(d, g * LANES)
            w_cat = bf(jnp.concatenate([w_in[:, :q_dim], w_in[:, q_dim + 3 * n_heads:], wg], axis=1))
            q, q_mem, gates = _mix_b(h, mix_norm[layer], w_cat, tables, q_dim=q_dim, mem_dim=mem_dim,
                                     scale=HEAD_DIM ** -0.5 * math.log2(math.e))
            cmp_kv, kvh, ks_aug, vs_aug, overlap = kv_side
            ocw, sel_bias = _nsa_cmp(q, gates, cmp_kv, overlap, kvh)
            tok = _nsa_sel(q, gates, sel_bias, ks_aug, vs_aug, ocw)

        mem_kv = _proj(mem2, mem_norm[layer], bf(w_mem_kv[layer]))
        mem_o = _mem_attn(q_mem, mem_kv)
        n_tok = tok.shape[1]
        h = _out_proj(h, tok, mem_o, bf(w_out[layer, :n_tok]), bf(w_out[layer, n_tok:]))

        last = layer == depth - 1
        h = _ffn(h, ffn_norm[layer, 1], bf(ffn_w_gate[layer, 1]), bf(ffn_w_up[layer, 1]), bf(ffn_w_down[layer, 1]),
                 final_norm if last else None)

    return h[None]
```

```python
import functools
import math

import jax
import jax.numpy as jnp
from jax import lax
from jax.experimental import pallas as pl
from jax.experimental.pallas import tpu as pltpu

HEAD_DIM = 128
ROT_DIM = HEAD_DIM // 4
ROPE_THETA = 500000.0
MEM_HEADS = 4
NSA_KV_GROUPS = 4
CMP_STRIDE = 16
CMP_BLOCK = 2 * CMP_STRIDE
SLC_BLOCK = 64
N_SELECT = 16
N_FORCED = 3
WINDOW = 512
RMS_EPS = 1e-6
CONV_K = 3

LANES = 128
SUBLANES = 8
NEG_BIG = -1e30
VMEM_LIMIT = 56 * 1024 * 1024

F32 = jnp.float32
BF16 = jnp.bfloat16


def _dot(a, b):
    return jnp.dot(a, b, preferred_element_type=F32)


def _dot_nt(a, b):
    return lax.dot_general(a, b, (((1,), (1,)), ((), ())), preferred_element_type=F32)


def _rms(x, g):
    ms = jnp.mean(x * x, axis=-1, keepdims=True)
    return x * lax.rsqrt(ms + RMS_EPS) * g


def _params(sem):
    return pltpu.CompilerParams(dimension_semantics=sem, vmem_limit_bytes=VMEM_LIMIT)


def _rope_table_kernel(pos_ref, inv_ref, cos_ref, sin_ref):
    ang = pos_ref[...].astype(F32) * inv_ref[...]
    lane = lax.broadcasted_iota(jnp.int32, ang.shape, 1)
    c = jnp.cos(ang)
    s = jnp.sin(ang)
    half = ROT_DIM // 2
    cos_ref[...] = jnp.where(lane < ROT_DIM, c, 1.0)
    sin_ref[...] = jnp.where(lane < half, -s, jnp.where(lane < ROT_DIM, s, 0.0))


def _rope_tables(positions_col):
    s = positions_col.shape[0]
    half = ROT_DIM // 2
    inv = 1.0 / (ROPE_THETA ** (jnp.arange(half, dtype=F32) / half))
    inv_row = jnp.concatenate([inv, inv, jnp.zeros((LANES - ROT_DIM,), F32)])[None, :]
    tm = min(s, 1024)
    return pl.pallas_call(
        _rope_table_kernel,
        out_shape=(jax.ShapeDtypeStruct((s, LANES), F32), jax.ShapeDtypeStruct((s, LANES), F32)),
        grid=(s // tm,),
        in_specs=[pl.BlockSpec((tm, 1), lambda i: (i, 0)), pl.BlockSpec((1, LANES), lambda i: (0, 0))],
        out_specs=(pl.BlockSpec((tm, LANES), lambda i: (i, 0)), pl.BlockSpec((tm, LANES), lambda i: (i, 0))),
        compiler_params=_params(("arbitrary",)),
        name="rope_tables",
    )(positions_col, inv_row)


def _rotate(x, cos_t, sin_t):
    half = ROT_DIM // 2
    lane = lax.broadcasted_iota(jnp.int32, x.shape, 1)
    swapped = jnp.where(lane < half, pltpu.roll(x, LANES - half, axis=1), pltpu.roll(x, half, axis=1))
    return x * cos_t + swapped * sin_t


def _store_heads(o_ref, acc, tables):
    for h in range(acc.shape[1] // HEAD_DIM):
        xh = acc[:, h * HEAD_DIM:(h + 1) * HEAD_DIM]
        if tables is not None:
            xh = _rotate(xh, tables[0][...], tables[1][...])
        o_ref[h] = xh.astype(o_ref.dtype)


def _ffn_kernel(x_ref, g_ref, wg_ref, wu_ref, wd_ref, *rest, final):
    if final:
        fg_ref, o_ref, xn_ref = rest
    else:
        o_ref, xn_ref = rest
    j = pl.program_id(1)

    @pl.when(j == 0)
    def _():
        x = x_ref[...]
        xn_ref[...] = _rms(x, g_ref[...]).astype(BF16)
        o_ref[...] = x

    xn = xn_ref[...]
    a = _dot(xn, wg_ref[...].astype(BF16))
    b = _dot(xn, wu_ref[...].astype(BF16))
    mid = (a * jax.nn.sigmoid(a) * b).astype(BF16)
    o_ref[...] += 0.5 * _dot(mid, wd_ref[...].astype(BF16))

    if final:
        @pl.when(j == pl.num_programs(1) - 1)
        def _():
            o_ref[...] = _rms(o_ref[...], fg_ref[...])


def _ffn(x, g, w_gate, w_up, w_down, layer, half, final_g=None, *, tm=1024, tf=256):
    s, d = x.shape
    f = w_gate.shape[3]
    tm = min(tm, s)
    final = final_g is not None
    in_specs = [
        pl.BlockSpec((tm, d), lambda i, j: (i, 0), pipeline_mode=pl.Buffered(1)),
        pl.BlockSpec((1, d), lambda i, j: (0, 0)),
        pl.BlockSpec((None, None, d, tf), lambda i, j: (layer, half, 0, j)),
        pl.BlockSpec((None, None, d, tf), lambda i, j: (layer, half, 0, j)),
        pl.BlockSpec((None, None, tf, d), lambda i, j: (layer, half, j, 0)),
    ]
    args = [x, g[None, :], w_gate, w_up, w_down]
    if final:
        in_specs.append(pl.BlockSpec((1, d), lambda i, j: (0, 0)))
        args.append(final_g[None, :])
    return pl.pallas_call(
        functools.partial(_ffn_kernel, final=final),
        out_shape=jax.ShapeDtypeStruct((s, d), F32),
        grid=(s // tm, f // tf),
        in_specs=in_specs,
        out_specs=pl.BlockSpec((tm, d), lambda i, j: (i, 0)),
        scratch_shapes=[pltpu.VMEM((tm, d), BF16)],
        compiler_params=_params(("arbitrary", "arbitrary")),
        name="ffn_final" if final else "ffn",
    )(*args)


def _mem_kv_kernel(x_ref, g_ref, w_ref, o_ref):
    xn = _rms(x_ref[...], g_ref[...]).astype(BF16)
    o_ref[...] = _dot(xn, w_ref[...].astype(BF16)).astype(o_ref.dtype)


def _mem_kv(mem2, g, w_mem_kv, layer, *, tn=512):
    n_mem, d = mem2.shape
    n = w_mem_kv.shape[2]
    return pl.pallas_call(
        _mem_kv_kernel,
        out_shape=jax.ShapeDtypeStruct((n_mem, n), BF16),
        grid=(n // tn,),
        in_specs=[
            pl.BlockSpec((n_mem, d), lambda j: (0, 0)),
            pl.BlockSpec((1, d), lambda j: (0, 0)),
            pl.BlockSpec((None, d, tn), lambda j: (layer, 0, j)),
        ],
        out_specs=pl.BlockSpec((n_mem, tn), lambda j: (0, j)),
        compiler_params=_params(("arbitrary",)),
        name="mem_kv",
    )(mem2, g[None, :], w_mem_kv)


def _kv_side_kernel(x_ref, g_ref, w_ref, cos_ref, sin_ref, chunk_ref, ks_ref, vs_ref, kw_ref, vw_ref,
                    xn_ref, stage_ref, *, n_grp):
    i = pl.program_id(0)
    j = pl.program_id(1)
    tm = x_ref.shape[0]

    @pl.when(j == 0)
    def _():
        xn_ref[...] = _rms(x_ref[...], g_ref[...]).astype(BF16)

    acc = _dot(xn_ref[...], w_ref[...].astype(BF16))

    def head(h, rot):
        xh = acc[:, h * HEAD_DIM:(h + 1) * HEAD_DIM]
        return _rotate(xh, cos_ref[...], sin_ref[...]) if rot else xh

    def store_chunks(first, rot):
        for h in range(n_grp):
            stage_ref[...] = head(h, rot)
            for p in range(CMP_STRIDE):
                chunk_ref[first + h, :, p * HEAD_DIM:(p + 1) * HEAD_DIM] = (
                    stage_ref[pl.ds(p, tm // CMP_STRIDE, stride=CMP_STRIDE), :].astype(chunk_ref.dtype))

    @pl.when(j == 0)
    def _():
        store_chunks(0, True)

    @pl.when(j == 1)
    def _():
        store_chunks(n_grp, False)

    @pl.when(j == 2)
    def _():
        n_slc = ks_ref.shape[2] - HEAD_DIM
        row_blk = (i * tm + lax.broadcasted_iota(jnp.int32, (tm, n_slc), 0)) // SLC_BLOCK
        code = jnp.where(row_blk == lax.broadcasted_iota(jnp.int32, (tm, n_slc), 1), 1.0, 0.0).astype(ks_ref.dtype)
        for h in range(n_grp):
            ks_ref[h, :, 0:HEAD_DIM] = head(h, True).astype(ks_ref.dtype)
            ks_ref[h, :, HEAD_DIM:] = code

    @pl.when(j == 3)
    def _():
        for h in range(n_grp):
            vs_ref[h, :, 0:HEAD_DIM] = head(h, False).astype(vs_ref.dtype)
            vs_ref[h, :, HEAD_DIM:] = jnp.ones((tm, HEAD_DIM), vs_ref.dtype)

    @pl.when(j == 4)
    def _():
        for h in range(n_grp):
            kw_ref[h] = head(h, True).astype(kw_ref.dtype)

    @pl.when(j == 5)
    def _():
        for h in range(n_grp):
            vw_ref[h] = head(h, False).astype(vw_ref.dtype)


def _kv_side(x, g, w_kv, tables, *, tm=1024):
    s, d = x.shape
    grp = NSA_KV_GROUPS
    tn = grp * HEAD_DIM
    assert w_kv.shape[1] == 6 * tn
    tm = min(tm, s)
    n_slc = s // SLC_BLOCK
    nc = s // CMP_STRIDE
    cw = CMP_STRIDE * HEAD_DIM
    row = lambda i, j: (0, i, 0)
    return pl.pallas_call(
        functools.partial(_kv_side_kernel, n_grp=grp),
        out_shape=(jax.ShapeDtypeStruct((2 * grp, nc, cw), BF16),
                   jax.ShapeDtypeStruct((grp, s, HEAD_DIM + n_slc), BF16),
                   jax.ShapeDtypeStruct((grp, s, 2 * HEAD_DIM), BF16),
                   jax.ShapeDtypeStruct((grp, s, HEAD_DIM), BF16),
                   jax.ShapeDtypeStruct((grp, s, HEAD_DIM), BF16)),
        grid=(s // tm, 6),
        in_specs=[
            pl.BlockSpec((tm, d), lambda i, j: (i, 0)),
            pl.BlockSpec((1, d), lambda i, j: (0, 0)),
            pl.BlockSpec((d, tn), lambda i, j: (0, j)),
            pl.BlockSpec((tm, LANES), lambda i, j: (i, 0)),
            pl.BlockSpec((tm, LANES), lambda i, j: (i, 0)),
        ],
        out_specs=(pl.BlockSpec((2 * grp, tm // CMP_STRIDE, cw), row),
                   pl.BlockSpec((grp, tm, HEAD_DIM + n_slc), row),
                   pl.BlockSpec((grp, tm, 2 * HEAD_DIM), row),
                   pl.BlockSpec((grp, tm, HEAD_DIM), row),
                   pl.BlockSpec((grp, tm, HEAD_DIM), row)),
        scratch_shapes=[pltpu.VMEM((tm, d), BF16), pltpu.VMEM((tm, HEAD_DIM), F32)],
        compiler_params=_params(("arbitrary", "arbitrary")),
        name="kv_side",
    )(x, g[None, :], w_kv, *tables)


def _mix_a_kernel(x_ref, g_ref, wb_ref, wc_ref, wv_ref, wq_ref, cw_ref, tok_ref, qmem_ref,
                  xn_ref, halo_ref, ext_ref):
    i = pl.program_id(0)
    j = pl.program_id(1)
    tm = x_ref.shape[0]
    tc = tok_ref.shape[1]

    @pl.when(j == 0)
    def _():
        xn_ref[...] = _rms(x_ref[...], g_ref[...]).astype(BF16)
        qmem_ref[...] = _dot(xn_ref[...], wq_ref[...].astype(BF16)).astype(qmem_ref.dtype)

    xn = xn_ref[...]
    gate_b = _dot(xn, wb_ref[...].astype(BF16))
    p = _dot(xn, wc_ref[...].astype(BF16)) * _dot(xn, wv_ref[...].astype(BF16))

    @pl.when(i == 0)
    def _():
        halo_ref[j] = jnp.zeros((SUBLANES, tc), F32)

    ext_ref[0:SUBLANES, :] = halo_ref[j]
    ext_ref[SUBLANES:, :] = p
    halo_ref[j] = p[tm - SUBLANES:, :]
    y = cw_ref[CONV_K - 1:CONV_K, :] * p
    for k in range(CONV_K - 1):
        off = SUBLANES - (CONV_K - 1 - k)
        y = y + cw_ref[k:k + 1, :] * ext_ref[off:off + tm, :]
    tok_ref[...] = (gate_b * y).astype(tok_ref.dtype)


def _mix_a(x, g, w_in_conv, conv_w, layer, *, tm=1024, tc=256):
    s, d = x.shape
    n_ch = conv_w.shape[2]
    nt = n_ch // tc
    md = w_in_conv.shape[2] - 3 * n_ch
    assert (3 * n_ch) % md == 0
    tm = min(tm, s)
    w_tile = lambda part: pl.BlockSpec((None, d, tc), lambda i, j: (layer, 0, part * nt + j))
    return pl.pallas_call(
        _mix_a_kernel,
        out_shape=(jax.ShapeDtypeStruct((s, n_ch), BF16), jax.ShapeDtypeStruct((s, md), BF16)),
        grid=(s // tm, nt),
        in_specs=[
            pl.BlockSpec((tm, d), lambda i, j: (i, 0)),
            pl.BlockSpec((1, d), lambda i, j: (0, 0)),
            w_tile(0), w_tile(1), w_tile(2),
            pl.BlockSpec((None, d, md), lambda i, j: (layer, 0, 3 * n_ch // md), pipeline_mode=pl.Buffered(1)),
            pl.BlockSpec((None, CONV_K, tc), lambda i, j: (layer, 0, j)),
        ],
        out_specs=(pl.BlockSpec((tm, tc), lambda i, j: (i, j)), pl.BlockSpec((tm, md), lambda i, j: (i, 0))),
        scratch_shapes=[pltpu.VMEM((tm, d), BF16), pltpu.VMEM((nt, SUBLANES, tc), F32),
                        pltpu.VMEM((tm + SUBLANES, tc), F32)],
        compiler_params=_params(("arbitrary", "arbitrary")),
        name="mix_a_in",
    )(x, g[None, :], w_in_conv, w_in_conv, w_in_conv, w_in_conv, conv_w)


def _mix_b_kernel(x_ref, g_ref, w_ref, cos_ref, sin_ref, q_ref, qmem_ref, gate_ref, xn_ref, *, q_tiles, m_tiles, scale):
    j = pl.program_id(1)

    @pl.when(j == 0)
    def _():
        xn_ref[...] = _rms(x_ref[...], g_ref[...]).astype(BF16)

    acc = _dot(xn_ref[...], w_ref[...])

    @pl.when(j < q_tiles)
    def _():
        _store_heads(q_ref, acc * scale, (cos_ref, sin_ref))

    @pl.when((j >= q_tiles) & (j < q_tiles + m_tiles))
    def _():
        qmem_ref[...] = acc.astype(qmem_ref.dtype)

    @pl.when(j >= q_tiles + m_tiles)
    def _():
        gate_ref[...] = jax.nn.sigmoid(acc)


def _mix_b(x, g, w, tables, *, q_dim, mem_dim, scale, tm=1024, tn=512):
    s, d = x.shape
    n = w.shape[1]
    q_tiles = q_dim // tn
    m_tiles = mem_dim // tn
    hp = tn // HEAD_DIM
    return pl.pallas_call(
        functools.partial(_mix_b_kernel, q_tiles=q_tiles, m_tiles=m_tiles, scale=scale),
        out_shape=(jax.ShapeDtypeStruct((q_dim // HEAD_DIM, s, HEAD_DIM), BF16),
                   jax.ShapeDtypeStruct((s, mem_dim), BF16),
                   jax.ShapeDtypeStruct((s, n - q_dim - mem_dim), F32)),
        grid=(s // tm, n // tn),
        in_specs=[
            pl.BlockSpec((tm, d), lambda i, j: (i, 0)),
            pl.BlockSpec((1, d), lambda i, j: (0, 0)),
            pl.BlockSpec((d, tn), lambda i, j: (0, j)),
            pl.BlockSpec((tm, LANES), lambda i, j: (i, 0)),
            pl.BlockSpec((tm, LANES), lambda i, j: (i, 0)),
        ],
        out_specs=(
            pl.BlockSpec((hp, tm, HEAD_DIM), lambda i, j: (jnp.minimum(j, q_tiles - 1), i, 0)),
            pl.BlockSpec((tm, tn), lambda i, j: (i, jnp.clip(j - q_tiles, 0, m_tiles - 1))),
            pl.BlockSpec((tm, tn), lambda i, j: (i, jnp.maximum(j - q_tiles - m_tiles, 0))),
        ),
        scratch_shapes=[pltpu.VMEM((tm, d), BF16)],
        compiler_params=_params(("arbitrary", "arbitrary")),
        name="mix_b_in",
    )(x, g[None, :], w, *tables)


def _mem_attn_kernel(q_ref, k_ref, v_ref, o_ref):
    scale = HEAD_DIM ** -0.5
    for h in range(MEM_HEADS):
        sl = slice(h * HEAD_DIM, (h + 1) * HEAD_DIM)
        s = _dot_nt(q_ref[:, sl], k_ref[:, sl]) * scale
        m = jnp.max(s, axis=-1, keepdims=True)
        e = jnp.exp(s - m)
        p = e / jnp.sum(e, axis=-1, keepdims=True)
        o_ref[:, sl] = _dot(p.astype(BF16), v_ref[:, sl]).astype(o_ref.dtype)


def _mem_attn(q, kv, *, tm=512):
    s, md = q.shape
    n_mem = kv.shape[0]
    return pl.pallas_call(
        _mem_attn_kernel,
        out_shape=jax.ShapeDtypeStruct((s, md), BF16),
        grid=(s // tm,),
        in_specs=[
            pl.BlockSpec((tm, md), lambda i: (i, 0)),
            pl.BlockSpec((n_mem, md), lambda i: (0, 0)),
            pl.BlockSpec((n_mem, md), lambda i: (0, 1)),
        ],
        out_specs=pl.BlockSpec((tm, md), lambda i: (i, 0)),
        compiler_params=_params(("arbitrary",)),
        name="mem_attn",
    )(q, kv, kv)


def _out_proj_kernel(h_ref, tok_ref, mem_ref, w1_ref, w2_ref, o_ref):
    o_ref[...] = (h_ref[...] + _dot(tok_ref[...], w1_ref[...].astype(BF16))
                  + _dot(mem_ref[...], w2_ref[...].astype(BF16)))


def _out_proj(h, tok, mem_o, w_out, layer, *, tm=1024, tn=512):
    s, d = h.shape
    n_tok = tok.shape[1]
    md = mem_o.shape[1]
    assert n_tok % md == 0
    tm = min(tm, s)
    return pl.pallas_call(
        _out_proj_kernel,
        out_shape=jax.ShapeDtypeStruct((s, d), F32),
        grid=(s // tm, d // tn),
        in_specs=[
            pl.BlockSpec((tm, tn), lambda i, j: (i, j)),
            pl.BlockSpec((tm, n_tok), lambda i, j: (i, 0)),
            pl.BlockSpec((tm, md), lambda i, j: (i, 0)),
            pl.BlockSpec((None, n_tok, tn), lambda i, j: (layer, 0, j)),
            pl.BlockSpec((None, md, tn), lambda i, j: (layer, n_tok // md, j)),
        ],
        out_specs=pl.BlockSpec((tm, tn), lambda i, j: (i, j)),
        compiler_params=_params(("arbitrary", "arbitrary")),
        name="out_proj",
    )(h, tok, mem_o, w_out, w_out)


def _compress_kernel(c_ref, pos_ref, w1_ref, w2_ref, o_ref):
    x = c_ref[0]
    nc, half = x.shape
    w1a = w1_ref[0, 0:half, :]
    w1b = w1_ref[0, half:, :]
    pos = jnp.broadcast_to(pos_ref[0], (SUBLANES, 2 * half))
    bias = _dot(pos, w1_ref[0])[0:1, :]
    first = _dot(x, w1a)
    second = pltpu.roll(_dot(x, w1b), nc - 1, axis=0)
    hid = jax.nn.gelu(first + second + bias, approximate=True)
    out = _dot(hid.astype(BF16), w2_ref[0])
    row = lax.broadcasted_iota(jnp.int32, out.shape, 0)
    o_ref[0] = jnp.where(row < nc - 1, out, 0.0).astype(o_ref.dtype)


def _compress(chunks, pos, w1, w2):
    n, nc, cw = chunks.shape
    g = n // 2
    return pl.pallas_call(
        _compress_kernel,
        out_shape=jax.ShapeDtypeStruct((2 * g, nc, HEAD_DIM), BF16),
        grid=(2 * g,),
        in_specs=[
            pl.BlockSpec((1, nc, cw), lambda i: (i, 0, 0)),
            pl.BlockSpec((1, 1, 2 * cw), lambda i: (i // g, 0, 0)),
            pl.BlockSpec((1, 2 * cw, HEAD_DIM), lambda i: (i // g, 0, 0)),
            pl.BlockSpec((1, HEAD_DIM, HEAD_DIM), lambda i: (i // g, 0, 0)),
        ],
        out_specs=pl.BlockSpec((1, nc, HEAD_DIM), lambda i: (i, 0, 0)),
        compiler_params=_params(("arbitrary",)),
        name="compress",
    )(chunks, pos, w1, w2)


def _masked_softmax2(s, mask):
    s = jnp.where(mask, s, NEG_BIG)
    m = jnp.max(s, axis=-1, keepdims=True)
    e = jnp.where(mask, jnp.exp2(s - m), 0.0)
    d = jnp.sum(e, axis=-1, keepdims=True)
    return e / jnp.where(d > 0, d, 1.0)


def _nsa_cmp_kernel(q_ref, gate_ref, kc_ref, vc_ref, ov_ref, kw_ref, vw_ref, ocw_ref, bias_ref,
                    *, tq, tw, rep, n_pick):
    i = pl.program_id(1)
    t0 = i * tq
    rows = rep * tq
    q3 = q_ref[...].reshape(rows, HEAD_DIM)
    gates = gate_ref[...]

    kc = kc_ref[0]
    s_c = _dot_nt(q3, kc)
    cmp_end = lax.broadcasted_iota(jnp.int32, s_c.shape, 1) * CMP_STRIDE + (CMP_BLOCK - 1)
    t_c = t0 + (lax.broadcasted_iota(jnp.int32, s_c.shape, 0) & (tq - 1))
    p_c = _masked_softmax2(s_c, cmp_end <= t_c)
    o_c = _dot(p_c.astype(BF16), vc_ref[0])

    p_sum = p_c[0:tq]
    for r in range(1, rep):
        p_sum = p_sum + p_c[r * tq:(r + 1) * tq]
    p_hi = p_sum.astype(BF16)
    p_lo = (p_sum - p_hi.astype(F32)).astype(BF16)
    imp = _dot(p_hi, ov_ref[...]) + _dot(p_lo, ov_ref[...])
    n_slc = imp.shape[1]
    blk = lax.broadcasted_iota(jnp.int32, imp.shape, 1)
    jt = (t0 + lax.broadcasted_iota(jnp.int32, imp.shape, 0)) // SLC_BLOCK
    valid = blk <= jt
    forced = (blk == 0) | (blk == jt) | (blk == jt - 1)
    work = jnp.where(valid, jnp.where(forced, -jnp.inf, imp), -jnp.inf)
    blk_f = blk.astype(F32)
    sel_bias = jnp.where(forced, 0.0, NEG_BIG)
    for _ in range(n_pick):
        mx = jnp.max(work, axis=-1, keepdims=True)
        first = jnp.min(jnp.where(work == mx, blk_f, float(n_slc)), axis=-1, keepdims=True)
        pick = blk_f == first
        sel_bias = jnp.where(pick, 0.0, sel_bias)
        work = jnp.where(pick, -jnp.inf, work)
    bias_ref[0] = jnp.where(valid, sel_bias, NEG_BIG).astype(bias_ref.dtype)

    wlen = tw + WINDOW
    for sub in range(tq // tw):
        ts0 = t0 + sub * tw
        ws = pl.multiple_of(jnp.maximum(ts0 - WINDOW, 0), tw)
        qs = jnp.concatenate([q_ref[r, sub * tw:(sub + 1) * tw, :] for r in range(rep)], axis=0)
        s_w = _dot_nt(qs, kw_ref[0, pl.ds(ws, wlen), :])
        kt_w = ws + lax.broadcasted_iota(jnp.int32, s_w.shape, 1)
        t_w = ts0 + (lax.broadcasted_iota(jnp.int32, s_w.shape, 0) & (tw - 1))
        p_w = _masked_softmax2(s_w, (kt_w <= t_w) & (kt_w > t_w - WINDOW))
        o_w = _dot(p_w.astype(BF16), vw_ref[0, pl.ds(ws, wlen), :])
        gs = gates[sub * tw:(sub + 1) * tw]
        for r in range(rep):
            oc_r = o_c[r * tq + sub * tw:r * tq + (sub + 1) * tw]
            ow_r = o_w[r * tw:(r + 1) * tw]
            ocw_ref[sub * tw:(sub + 1) * tw, r * HEAD_DIM:(r + 1) * HEAD_DIM] = (
                gs[:, 3 * r:3 * r + 1] * oc_r + gs[:, 3 * r + 2:3 * r + 3] * ow_r)


def _nsa_cmp(q, gates, cmp_kv, overlap, kw, vw, *, tq=512, tw=128):
    n_h, s, _ = q.shape
    g = NSA_KV_GROUPS
    rep = n_h // g
    nc = cmp_kv.shape[1]
    n_slc = s // SLC_BLOCK
    tq = min(tq, s)
    n_pick = max(min(N_SELECT, n_slc) - N_FORCED, 0)
    kernel = functools.partial(_nsa_cmp_kernel, tq=tq, tw=tw, rep=rep, n_pick=n_pick)
    return pl.pallas_call(
        kernel,
        out_shape=(jax.ShapeDtypeStruct((s, n_h * HEAD_DIM), F32), jax.ShapeDtypeStruct((g, s, n_slc), BF16)),
        grid=(g, s // tq),
        in_specs=[
            pl.BlockSpec((rep, tq, HEAD_DIM), lambda gi, i: (gi, i, 0)),
            pl.BlockSpec((tq, LANES), lambda gi, i: (i, gi)),
            pl.BlockSpec((1, nc, HEAD_DIM), lambda gi, i: (gi, 0, 0)),
            pl.BlockSpec((1, nc, HEAD_DIM), lambda gi, i: (g + gi, 0, 0)),
            pl.BlockSpec((nc, n_slc), lambda gi, i: (0, 0)),
            pl.BlockSpec((1, s, HEAD_DIM), lambda gi, i: (gi, 0, 0)),
            pl.BlockSpec((1, s, HEAD_DIM), lambda gi, i: (gi, 0, 0)),
        ],
        out_specs=(pl.BlockSpec((tq, rep * HEAD_DIM), lambda gi, i: (i, gi)),
                   pl.BlockSpec((1, tq, n_slc), lambda gi, i: (gi, i, 0))),
        compiler_params=_params(("arbitrary", "arbitrary")),
        name="nsa_cmp_win",
    )(q, gates, cmp_kv, cmp_kv, overlap, kw, vw)


def _nsa_sel_kernel(q_ref, bias_ref, ks_ref, vs_ref, ocw_ref, gate_ref, o_ref, m_ref, acc_ref, sa_ref, sb_ref,
                    *, tq, tk, rep):
    i = pl.program_id(1)
    t0 = i * tq
    rows = rep * tq
    q3 = q_ref[...].reshape(rows, HEAD_DIM)
    q_aug = jnp.concatenate([q3, jnp.concatenate([bias_ref[0]] * rep, axis=0)], axis=1)

    m_ref[...] = jnp.full(m_ref.shape, NEG_BIG, F32)
    acc_ref[...] = jnp.zeros(acc_ref.shape, F32)

    def scores(kt, s_ref):
        k_aug = ks_ref[0, pl.ds(pl.multiple_of(kt * tk, tk), tk), :]
        for r in range(rep):
            rs = slice(r * tq, (r + 1) * tq)
            s_ref[rs, :] = _dot_nt(q_aug[rs], k_aug)

    def consume(kt, s_ref, causal):
        k0 = pl.multiple_of(kt * tk, tk)
        v_aug = vs_ref[0, pl.ds(k0, tk), :]
        for r in range(rep):
            rs = slice(r * tq, (r + 1) * tq)
            s = s_ref[rs, :]
            if causal:
                t_s = t0 + lax.broadcasted_iota(jnp.int32, s.shape, 0)
                s = jnp.where(lax.broadcasted_iota(jnp.int32, s.shape, 1) + k0 <= t_s, s, NEG_BIG)
            m_prev = m_ref[rs, :]
            m_new = jnp.maximum(m_prev, jnp.max(s, axis=-1, keepdims=True))
            alpha = jnp.exp2(m_prev - m_new)
            p = jnp.exp2(s - jnp.concatenate([m_new] * (tk // LANES), axis=1))
            acc_ref[rs, :] = (jnp.concatenate([alpha, alpha], axis=1) * acc_ref[rs, :]
                              + _dot(p.astype(BF16), v_aug))
            m_ref[rs, :] = m_new

    last = (t0 + tq + tk - 1) // tk - 1
    n_pairs = last // 2

    scores(0, sa_ref)

    def pair(a, carry):
        kt = 2 * a
        scores(kt + 1, sb_ref)
        consume(kt, sa_ref, False)
        scores(kt + 2, sa_ref)
        consume(kt + 1, sb_ref, False)
        return carry

    lax.fori_loop(0, n_pairs, pair, 0)

    @pl.when(last == 2 * n_pairs)
    def _():
        consume(last, sa_ref, True)

    @pl.when(last != 2 * n_pairs)
    def _():
        scores(last, sb_ref)
        consume(last - 1, sa_ref, False)
        consume(last, sb_ref, True)

    acc = acc_ref[...]
    o_s = acc[:, :HEAD_DIM] / acc[:, HEAD_DIM:]
    gates = gate_ref[...]
    for r in range(rep):
        cs = slice(r * HEAD_DIM, (r + 1) * HEAD_DIM)
        o_ref[:, cs] = (ocw_ref[:, cs] + gates[:, 3 * r + 1:3 * r + 2] * o_s[r * tq:(r + 1) * tq]).astype(o_ref.dtype)


def _nsa_sel(q, gates, sel_bias, ks_aug, vs_aug, ocw, *, tq=512, tk=512):
    n_h, s, _ = q.shape
    g = NSA_KV_GROUPS
    rep = n_h // g
    n_slc = sel_bias.shape[2]
    tk = min(tk, s)
    tq = min(tq, tk)
    rows = rep * tq
    kernel = functools.partial(_nsa_sel_kernel, tq=tq, tk=tk, rep=rep)
    return pl.pallas_call(
        kernel,
        out_shape=jax.ShapeDtypeStruct((s, n_h * HEAD_DIM), BF16),
        grid=(g, s // tq),
        in_specs=[
            pl.BlockSpec((rep, tq, HEAD_DIM), lambda gi, i: (gi, i, 0)),
            pl.BlockSpec((1, tq, n_slc), lambda gi, i: (gi, i, 0)),
            pl.BlockSpec((1, s, HEAD_DIM + n_slc), lambda gi, i: (gi, 0, 0)),
            pl.BlockSpec((1, s, 2 * HEAD_DIM), lambda gi, i: (gi, 0, 0)),
            pl.BlockSpec((tq, rep * HEAD_DIM), lambda gi, i: (i, gi)),
            pl.BlockSpec((tq, LANES), lambda gi, i: (i, gi)),
        ],
        out_specs=pl.BlockSpec((tq, rep * HEAD_DIM), lambda gi, i: (i, gi)),
        scratch_shapes=[pltpu.VMEM((rows, LANES), F32), pltpu.VMEM((rows, 2 * HEAD_DIM), F32),
                        pltpu.VMEM((rows, tk), F32), pltpu.VMEM((rows, tk), F32)],
        compiler_params=_params(("arbitrary", "arbitrary")),
        name="nsa_sel",
    )(q, sel_bias, ks_aug, vs_aug, ocw, gates)


def _cmp_to_slc(n_cmp, n_slc):
    cs = jnp.arange(n_cmp)[:, None] * CMP_STRIDE
    ss = jnp.arange(n_slc)[None, :] * SLC_BLOCK
    ov = jnp.clip(jnp.minimum(cs + CMP_BLOCK, ss + SLC_BLOCK) - jnp.maximum(cs, ss), 0, None)
    return (ov.astype(F32) / CMP_BLOCK).astype(BF16)


def kernel(x, mem, positions, ffn_norm, ffn_w_gate, ffn_w_up, ffn_w_down, mix_norm, mem_norm, w_mem_kv, w_out, w_in_conv, conv_w, w_in_nsa, kv_norm, w_kv, cmp_pos_k, cmp_w1_k, cmp_w2_k, cmp_pos_v, cmp_w1_v, cmp_w2_v, final_norm):
    b, s, d = x.shape
    assert b == 1
    depth = ffn_norm.shape[0]
    n_a = w_in_conv.shape[0]
    mem_dim = w_mem_kv.shape[2] // 2
    conv_ch = conv_w.shape[2]
    kv_w = w_kv.shape[1] // 6
    g = NSA_KV_GROUPS
    n_q = w_in_nsa.shape[2] - mem_dim
    n_heads = n_q // (HEAD_DIM + 3)
    q_dim = n_heads * HEAD_DIM
    rep = n_heads // g
    n_slc = s // SLC_BLOCK
    bf = lambda w: w.astype(BF16)

    h = x[0]
    mem2 = mem[0]
    tables = _rope_tables(positions.reshape(s, 1))

    kv_side = None
    for layer in range(depth):
        if layer == n_a:
            chunks, ks_aug, vs_aug, kw, vw = _kv_side(h, kv_norm, w_kv, tables)
            nc = s // CMP_STRIDE
            pos = bf(jnp.stack([cmp_pos_k, cmp_pos_v]).reshape(2, 1, CMP_BLOCK * HEAD_DIM))
            cmp_kv = _compress(chunks, pos, bf(jnp.stack([cmp_w1_k, cmp_w1_v])), bf(jnp.stack([cmp_w2_k, cmp_w2_v])))
            kv_side = (cmp_kv, ks_aug, vs_aug, kw, vw, _cmp_to_slc(nc, n_slc))

        h = _ffn(h, ffn_norm[layer, 0], ffn_w_gate, ffn_w_up, ffn_w_down, layer, 0)

        if layer < n_a:
            tok, q_mem = _mix_a(h, mix_norm[layer], w_in_conv, conv_w, layer)
        else:
            w_in = w_in_nsa[layer - n_a]
            wg = w_in[:, q_dim:q_dim + 3 * n_heads].reshape(d, g, 3 * rep)
            wg = jnp.pad(wg, ((0, 0), (0, 0), (0, LANES - 3 * rep))).reshape(d, g * LANES)
            w_cat = bf(jnp.concatenate([w_in[:, :q_dim], w_in[:, q_dim + 3 * n_heads:], wg], axis=1))
            q, q_mem, gates = _mix_b(h, mix_norm[layer], w_cat, tables, q_dim=q_dim, mem_dim=mem_dim,
                                     scale=HEAD_DIM ** -0.5 * math.log2(math.e))
            cmp_kv, ks_aug, vs_aug, kw, vw, overlap = kv_side
            ocw, sel_bias = _nsa_cmp(q, gates, cmp_kv, overlap, kw, vw)
            tok = _nsa_sel(q, gates, sel_bias, ks_aug, vs_aug, ocw)

        mem_o = _mem_attn(q_mem, _mem_kv(mem2, mem_norm[layer], w_mem_kv, layer))
        h = _out_proj(h, tok, mem_o, w_out, layer)

        last = layer == depth - 1
        h = _ffn(h, ffn_norm[layer, 1], ffn_w_gate, ffn_w_up, ffn_w_down, layer, 1, final_norm if last else None)

    return h[None]
```

```python
import functools
import math

import jax
import jax.numpy as jnp
from jax import lax
from jax.experimental import pallas as pl
from jax.experimental.pallas import tpu as pltpu

HEAD_DIM = 128
ROT_DIM = HEAD_DIM // 4
ROPE_THETA = 500000.0
MEM_HEADS = 4
NSA_KV_GROUPS = 4
CMP_STRIDE = 16
CMP_BLOCK = 2 * CMP_STRIDE
SLC_BLOCK = 64
N_SELECT = 16
N_FORCED = 3
WINDOW = 512
RMS_EPS = 1e-6
CONV_K = 3

LANES = 128
SUBLANES = 8
NEG_BIG = -1e30
VMEM_LIMIT = 56 * 1024 * 1024

F32 = jnp.float32
BF16 = jnp.bfloat16


def _dot(a, b):
    return jnp.dot(a, b, preferred_element_type=F32)


def _dot_nt(a, b):
    return lax.dot_general(a, b, (((1,), (1,)), ((), ())), preferred_element_type=F32)


def _rms(x, g):
    ms = jnp.mean(x * x, axis=-1, keepdims=True)
    return x * lax.rsqrt(ms + RMS_EPS) * g


def _params(sem):
    return pltpu.CompilerParams(dimension_semantics=sem, vmem_limit_bytes=VMEM_LIMIT)


def _rope_table_kernel(pos_ref, inv_ref, cos_ref, sin_ref):
    ang = pos_ref[...].astype(F32) * inv_ref[...]
    lane = lax.broadcasted_iota(jnp.int32, ang.shape, 1)
    c = jnp.cos(ang)
    s = jnp.sin(ang)
    half = ROT_DIM // 2
    cos_ref[...] = jnp.where(lane < ROT_DIM, c, 1.0)
    sin_ref[...] = jnp.where(lane < half, -s, jnp.where(lane < ROT_DIM, s, 0.0))


def _rope_tables(positions_col):
    s = positions_col.shape[0]
    half = ROT_DIM // 2
    inv = 1.0 / (ROPE_THETA ** (jnp.arange(half, dtype=F32) / half))
    inv_row = jnp.concatenate([inv, inv, jnp.zeros((LANES - ROT_DIM,), F32)])[None, :]
    tm = min(s, 1024)
    return pl.pallas_call(
        _rope_table_kernel,
        out_shape=(jax.ShapeDtypeStruct((s, LANES), F32), jax.ShapeDtypeStruct((s, LANES), F32)),
        grid=(s // tm,),
        in_specs=[pl.BlockSpec((tm, 1), lambda i: (i, 0)), pl.BlockSpec((1, LANES), lambda i: (0, 0))],
        out_specs=(pl.BlockSpec((tm, LANES), lambda i: (i, 0)), pl.BlockSpec((tm, LANES), lambda i: (i, 0))),
        compiler_params=_params(("arbitrary",)),
        name="rope_tables",
    )(positions_col, inv_row)


def _rotate(x, cos_t, sin_t):
    half = ROT_DIM // 2
    lane = lax.broadcasted_iota(jnp.int32, x.shape, 1)
    swapped = jnp.where(lane < half, pltpu.roll(x, LANES - half, axis=1), pltpu.roll(x, half, axis=1))
    return x * cos_t + swapped * sin_t


def _store_heads(o_ref, acc, tables):
    for h in range(acc.shape[1] // HEAD_DIM):
        xh = acc[:, h * HEAD_DIM:(h + 1) * HEAD_DIM]
        if tables is not None:
            xh = _rotate(xh, tables[0][...], tables[1][...])
        o_ref[h] = xh.astype(o_ref.dtype)


def _ffn_kernel(x_ref, g_ref, wg_ref, wu_ref, wd_ref, *rest, final):
    if final:
        fg_ref, o_ref, xn_ref = rest
    else:
        o_ref, xn_ref = rest
    j = pl.program_id(1)

    @pl.when(j == 0)
    def _():
        x = x_ref[...]
        xn_ref[...] = _rms(x, g_ref[...]).astype(BF16)
        o_ref[...] = x

    xn = xn_ref[...]
    a = _dot(xn, wg_ref[...].astype(BF16))
    b = _dot(xn, wu_ref[...].astype(BF16))
    mid = (a * jax.nn.sigmoid(a) * b).astype(BF16)
    o_ref[...] += 0.5 * _dot(mid, wd_ref[...].astype(BF16))

    if final:
        @pl.when(j == pl.num_programs(1) - 1)
        def _():
            o_ref[...] = _rms(o_ref[...], fg_ref[...])


def _ffn(x, g, w_gate, w_up, w_down, layer, half, final_g=None, *, tm=1024, tf=256):
    s, d = x.shape
    f = w_gate.shape[3]
    tm = min(tm, s)
    final = final_g is not None
    in_specs = [
        pl.BlockSpec((tm, d), lambda i, j: (i, 0)),
        pl.BlockSpec((1, d), lambda i, j: (0, 0)),
        pl.BlockSpec((None, None, d, tf), lambda i, j: (layer, half, 0, j)),
        pl.BlockSpec((None, None, d, tf), lambda i, j: (layer, half, 0, j)),
        pl.BlockSpec((None, None, tf, d), lambda i, j: (layer, half, j, 0)),
    ]
    args = [x, g[None, :], w_gate, w_up, w_down]
    if final:
        in_specs.append(pl.BlockSpec((1, d), lambda i, j: (0, 0)))
        args.append(final_g[None, :])
    return pl.pallas_call(
        functools.partial(_ffn_kernel, final=final),
        out_shape=jax.ShapeDtypeStruct((s, d), F32),
        grid=(s // tm, f // tf),
        in_specs=in_specs,
        out_specs=pl.BlockSpec((tm, d), lambda i, j: (i, 0)),
        scratch_shapes=[pltpu.VMEM((tm, d), BF16)],
        compiler_params=_params(("arbitrary", "arbitrary")),
        name="ffn_final" if final else "ffn",
    )(*args)


def _mem_kv_kernel(x_ref, g_ref, w_ref, o_ref):
    xn = _rms(x_ref[...], g_ref[...]).astype(BF16)
    o_ref[...] = _dot(xn, w_ref[...].astype(BF16)).astype(o_ref.dtype)


def _mem_kv(mem2, g, w_mem_kv, layer, *, tn=512):
    n_mem, d = mem2.shape
    n = w_mem_kv.shape[2]
    return pl.pallas_call(
        _mem_kv_kernel,
        out_shape=jax.ShapeDtypeStruct((n_mem, n), BF16),
        grid=(n // tn,),
        in_specs=[
            pl.BlockSpec((n_mem, d), lambda j: (0, 0)),
            pl.BlockSpec((1, d), lambda j: (0, 0)),
            pl.BlockSpec((None, d, tn), lambda j: (layer, 0, j)),
        ],
        out_specs=pl.BlockSpec((n_mem, tn), lambda j: (0, j)),
        compiler_params=_params(("arbitrary",)),
        name="mem_kv",
    )(mem2, g[None, :], w_mem_kv)


def _kv_side_kernel(x_ref, g_ref, w_ref, cos_ref, sin_ref, chunk_ref, ks_ref, vs_ref, kw_ref, vw_ref,
                    xn_ref, stage_ref, *, n_grp):
    i = pl.program_id(0)
    j = pl.program_id(1)
    tm = x_ref.shape[0]

    @pl.when(j == 0)
    def _():
        xn_ref[...] = _rms(x_ref[...], g_ref[...]).astype(BF16)

    acc = _dot(xn_ref[...], w_ref[...].astype(BF16))

    def head(h, rot):
        xh = acc[:, h * HEAD_DIM:(h + 1) * HEAD_DIM]
        return _rotate(xh, cos_ref[...], sin_ref[...]) if rot else xh

    def store_chunks(first, rot):
        for h in range(n_grp):
            stage_ref[...] = head(h, rot)
            for p in range(CMP_STRIDE):
                chunk_ref[first + h, :, p * HEAD_DIM:(p + 1) * HEAD_DIM] = (
                    stage_ref[pl.ds(p, tm // CMP_STRIDE, stride=CMP_STRIDE), :].astype(chunk_ref.dtype))

    @pl.when(j == 0)
    def _():
        store_chunks(0, True)

    @pl.when(j == 1)
    def _():
        store_chunks(n_grp, False)

    @pl.when(j == 2)
    def _():
        n_slc = ks_ref.shape[2] - HEAD_DIM
        row_blk = (i * tm + lax.broadcasted_iota(jnp.int32, (tm, n_slc), 0)) // SLC_BLOCK
        code = jnp.where(row_blk == lax.broadcasted_iota(jnp.int32, (tm, n_slc), 1), 1.0, 0.0).astype(ks_ref.dtype)
        for h in range(n_grp):
            ks_ref[h, :, 0:HEAD_DIM] = head(h, True).astype(ks_ref.dtype)
            ks_ref[h, :, HEAD_DIM:] = code

    @pl.when(j == 3)
    def _():
        for h in range(n_grp):
            vs_ref[h, :, 0:HEAD_DIM] = head(h, False).astype(vs_ref.dtype)
            vs_ref[h, :, HEAD_DIM:] = jnp.ones((tm, HEAD_DIM), vs_ref.dtype)

    @pl.when(j == 4)
    def _():
        for h in range(n_grp):
            kw_ref[h] = head(h, True).astype(kw_ref.dtype)

    @pl.when(j == 5)
    def _():
        for h in range(n_grp):
            vw_ref[h] = head(h, False).astype(vw_ref.dtype)


def _kv_side(x, g, w_kv, tables, *, tm=1024):
    s, d = x.shape
    grp = NSA_KV_GROUPS
    tn = grp * HEAD_DIM
    assert w_kv.shape[1] == 6 * tn
    tm = min(tm, s)
    n_slc = s // SLC_BLOCK
    nc = s // CMP_STRIDE
    cw = CMP_STRIDE * HEAD_DIM
    row = lambda i, j: (0, i, 0)
    return pl.pallas_call(
        functools.partial(_kv_side_kernel, n_grp=grp),
        out_shape=(jax.ShapeDtypeStruct((2 * grp, nc, cw), BF16),
                   jax.ShapeDtypeStruct((grp, s, HEAD_DIM + n_slc), BF16),
                   jax.ShapeDtypeStruct((grp, s, 2 * HEAD_DIM), BF16),
                   jax.ShapeDtypeStruct((grp, s, HEAD_DIM), BF16),
                   jax.ShapeDtypeStruct((grp, s, HEAD_DIM), BF16)),
        grid=(s // tm, 6),
        in_specs=[
            pl.BlockSpec((tm, d), lambda i, j: (i, 0)),
            pl.BlockSpec((1, d), lambda i, j: (0, 0)),
            pl.BlockSpec((d, tn), lambda i, j: (0, j)),
            pl.BlockSpec((tm, LANES), lambda i, j: (i, 0)),
            pl.BlockSpec((tm, LANES), lambda i, j: (i, 0)),
        ],
        out_specs=(pl.BlockSpec((2 * grp, tm // CMP_STRIDE, cw), row),
                   pl.BlockSpec((grp, tm, HEAD_DIM + n_slc), row),
                   pl.BlockSpec((grp, tm, 2 * HEAD_DIM), row),
                   pl.BlockSpec((grp, tm, HEAD_DIM), row),
                   pl.BlockSpec((grp, tm, HEAD_DIM), row)),
        scratch_shapes=[pltpu.VMEM((tm, d), BF16), pltpu.VMEM((tm, HEAD_DIM), F32)],
        compiler_params=_params(("arbitrary", "arbitrary")),
        name="kv_side",
    )(x, g[None, :], w_kv, *tables)


def _mix_a_kernel(x_ref, g_ref, wb_ref, wc_ref, wv_ref, wq_ref, cw_ref, tok_ref, qmem_ref,
                  xn_ref, halo_ref, ext_ref):
    i = pl.program_id(0)
    j = pl.program_id(1)
    tm = x_ref.shape[0]
    tc = tok_ref.shape[1]

    @pl.when(j == 0)
    def _():
        xn_ref[...] = _rms(x_ref[...], g_ref[...]).astype(BF16)
        qmem_ref[...] = _dot(xn_ref[...], wq_ref[...].astype(BF16)).astype(qmem_ref.dtype)

    xn = xn_ref[...]
    gate_b = _dot(xn, wb_ref[...].astype(BF16))
    p = _dot(xn, wc_ref[...].astype(BF16)) * _dot(xn, wv_ref[...].astype(BF16))

    @pl.when(i == 0)
    def _():
        halo_ref[j] = jnp.zeros((SUBLANES, tc), F32)

    ext_ref[0:SUBLANES, :] = halo_ref[j]
    ext_ref[SUBLANES:, :] = p
    halo_ref[j] = p[tm - SUBLANES:, :]
    y = cw_ref[CONV_K - 1:CONV_K, :] * p
    for k in range(CONV_K - 1):
        off = SUBLANES - (CONV_K - 1 - k)
        y = y + cw_ref[k:k + 1, :] * ext_ref[off:off + tm, :]
    tok_ref[...] = (gate_b * y).astype(tok_ref.dtype)


def _mix_a(x, g, w_in_conv, conv_w, layer, *, tm=1024, tc=256):
    s, d = x.shape
    n_ch = conv_w.shape[2]
    nt = n_ch // tc
    md = w_in_conv.shape[2] - 3 * n_ch
    assert (3 * n_ch) % md == 0
    tm = min(tm, s)
    w_tile = lambda part: pl.BlockSpec((None, d, tc), lambda i, j: (layer, 0, part * nt + j))
    return pl.pallas_call(
        _mix_a_kernel,
        out_shape=(jax.ShapeDtypeStruct((s, n_ch), BF16), jax.ShapeDtypeStruct((s, md), BF16)),
        grid=(s // tm, nt),
        in_specs=[
            pl.BlockSpec((tm, d), lambda i, j: (i, 0)),
            pl.BlockSpec((1, d), lambda i, j: (0, 0)),
            w_tile(0), w_tile(1), w_tile(2),
            pl.BlockSpec((None, d, md), lambda i, j: (layer, 0, 3 * n_ch // md), pipeline_mode=pl.Buffered(1)),
            pl.BlockSpec((None, CONV_K, tc), lambda i, j: (layer, 0, j)),
        ],
        out_specs=(pl.BlockSpec((tm, tc), lambda i, j: (i, j)), pl.BlockSpec((tm, md), lambda i, j: (i, 0))),
        scratch_shapes=[pltpu.VMEM((tm, d), BF16), pltpu.VMEM((nt, SUBLANES, tc), F32),
                        pltpu.VMEM((tm + SUBLANES, tc), F32)],
        compiler_params=_params(("arbitrary", "arbitrary")),
        name="mix_a_in",
    )(x, g[None, :], w_in_conv, w_in_conv, w_in_conv, w_in_conv, conv_w)


def _mix_b_kernel(x_ref, g_ref, w_ref, cos_ref, sin_ref, q_ref, qmem_ref, gate_ref, xn_ref, *, q_tiles, m_tiles, scale):
    j = pl.program_id(1)

    @pl.when(j == 0)
    def _():
        xn_ref[...] = _rms(x_ref[...], g_ref[...]).astype(BF16)

    acc = _dot(xn_ref[...], w_ref[...])

    @pl.when(j < q_tiles)
    def _():
        _store_heads(q_ref, acc * scale, (cos_ref, sin_ref))

    @pl.when((j >= q_tiles) & (j < q_tiles + m_tiles))
    def _():
        qmem_ref[...] = acc.astype(qmem_ref.dtype)

    @pl.when(j >= q_tiles + m_tiles)
    def _():
        gate_ref[...] = jax.nn.sigmoid(acc)


def _mix_b(x, g, w, tables, *, q_dim, mem_dim, scale, tm=1024, tn=512):
    s, d = x.shape
    n = w.shape[1]
    q_tiles = q_dim // tn
    m_tiles = mem_dim // tn
    hp = tn // HEAD_DIM
    return pl.pallas_call(
        functools.partial(_mix_b_kernel, q_tiles=q_tiles, m_tiles=m_tiles, scale=scale),
        out_shape=(jax.ShapeDtypeStruct((q_dim // HEAD_DIM, s, HEAD_DIM), BF16),
                   jax.ShapeDtypeStruct((s, mem_dim), BF16),
                   jax.ShapeDtypeStruct((s, n - q_dim - mem_dim), F32)),
        grid=(s // tm, n // tn),
        in_specs=[
            pl.BlockSpec((tm, d), lambda i, j: (i, 0)),
            pl.BlockSpec((1, d), lambda i, j: (0, 0)),
            pl.BlockSpec((d, tn), lambda i, j: (0, j)),
            pl.BlockSpec((tm, LANES), lambda i, j: (i, 0)),
            pl.BlockSpec((tm, LANES), lambda i, j: (i, 0)),
        ],
        out_specs=(
            pl.BlockSpec((hp, tm, HEAD_DIM), lambda i, j: (jnp.minimum(j, q_tiles - 1), i, 0)),
            pl.BlockSpec((tm, tn), lambda i, j: (i, jnp.clip(j - q_tiles, 0, m_tiles - 1))),
            pl.BlockSpec((tm, tn), lambda i, j: (i, jnp.maximum(j - q_tiles - m_tiles, 0))),
        ),
        scratch_shapes=[pltpu.VMEM((tm, d), BF16)],
        compiler_params=_params(("arbitrary", "arbitrary")),
        name="mix_b_in",
    )(x, g[None, :], w, *tables)


def _mem_attn_kernel(q_ref, k_ref, v_ref, o_ref):
    scale = HEAD_DIM ** -0.5
    for h in range(MEM_HEADS):
        sl = slice(h * HEAD_DIM, (h + 1) * HEAD_DIM)
        s = _dot_nt(q_ref[:, sl], k_ref[:, sl]) * scale
        m = jnp.max(s, axis=-1, keepdims=True)
        e = jnp.exp(s - m)
        p = e / jnp.sum(e, axis=-1, keepdims=True)
        o_ref[:, sl] = _dot(p.astype(BF16), v_ref[:, sl]).astype(o_ref.dtype)


def _mem_attn(q, kv, *, tm=512):
    s, md = q.shape
    n_mem = kv.shape[0]
    return pl.pallas_call(
        _mem_attn_kernel,
        out_shape=jax.ShapeDtypeStruct((s, md), BF16),
        grid=(s // tm,),
        in_specs=[
            pl.BlockSpec((tm, md), lambda i: (i, 0)),
            pl.BlockSpec((n_mem, md), lambda i: (0, 0)),
            pl.BlockSpec((n_mem, md), lambda i: (0, 1)),
        ],
        out_specs=pl.BlockSpec((tm, md), lambda i: (i, 0)),
        compiler_params=_params(("arbitrary",)),
        name="mem_attn",
    )(q, kv, kv)


def _out_proj_kernel(h_ref, tok_ref, mem_ref, w1_ref, w2_ref, o_ref):
    o_ref[...] = (h_ref[...] + _dot(tok_ref[...], w1_ref[...].astype(BF16))
                  + _dot(mem_ref[...], w2_ref[...].astype(BF16)))


def _out_proj(h, tok, mem_o, w_out, layer, *, tm=1024, tn=512):
    s, d = h.shape
    n_tok = tok.shape[1]
    md = mem_o.shape[1]
    assert n_tok % md == 0
    tm = min(tm, s)
    return pl.pallas_call(
        _out_proj_kernel,
        out_shape=jax.ShapeDtypeStruct((s, d), F32),
        grid=(s // tm, d // tn),
        in_specs=[
            pl.BlockSpec((tm, tn), lambda i, j: (i, j)),
            pl.BlockSpec((tm, n_tok), lambda i, j: (i, 0)),
            pl.BlockSpec((tm, md), lambda i, j: (i, 0)),
            pl.BlockSpec((None, n_tok, tn), lambda i, j: (layer, 0, j)),
            pl.BlockSpec((None, md, tn), lambda i, j: (layer, n_tok // md, j)),
        ],
        out_specs=pl.BlockSpec((tm, tn), lambda i, j: (i, j)),
        compiler_params=_params(("arbitrary", "arbitrary")),
        name="out_proj",
    )(h, tok, mem_o, w_out, w_out)


def _compress_kernel(c_ref, pos_ref, w1_ref, w2_ref, o_ref):
    x = c_ref[0]
    nc, half = x.shape
    w1a = w1_ref[0, 0:half, :]
    w1b = w1_ref[0, half:, :]
    pos = jnp.broadcast_to(pos_ref[0], (SUBLANES, 2 * half))
    bias = _dot(pos, w1_ref[0])[0:1, :]
    first = _dot(x, w1a)
    second = pltpu.roll(_dot(x, w1b), nc - 1, axis=0)
    hid = jax.nn.gelu(first + second + bias, approximate=True)
    out = _dot(hid.astype(BF16), w2_ref[0])
    row = lax.broadcasted_iota(jnp.int32, out.shape, 0)
    o_ref[0] = jnp.where(row < nc - 1, out, 0.0).astype(o_ref.dtype)


def _compress(chunks, pos, w1, w2):
    n, nc, cw = chunks.shape
    g = n // 2
    return pl.pallas_call(
        _compress_kernel,
        out_shape=jax.ShapeDtypeStruct((2 * g, nc, HEAD_DIM), BF16),
        grid=(2 * g,),
        in_specs=[
            pl.BlockSpec((1, nc, cw), lambda i: (i, 0, 0)),
            pl.BlockSpec((1, 1, 2 * cw), lambda i: (i // g, 0, 0)),
            pl.BlockSpec((1, 2 * cw, HEAD_DIM), lambda i: (i // g, 0, 0)),
            pl.BlockSpec((1, HEAD_DIM, HEAD_DIM), lambda i: (i // g, 0, 0)),
        ],
        out_specs=pl.BlockSpec((1, nc, HEAD_DIM), lambda i: (i, 0, 0)),
        compiler_params=_params(("arbitrary",)),
        name="compress",
    )(chunks, pos, w1, w2)


def _softmax2_parts(s, bias):
    s = s + bias
    e = jnp.exp2(s - jnp.max(s, axis=-1, keepdims=True))
    return e, 1.0 / jnp.sum(e, axis=-1, keepdims=True)


def _nsa_cmp_kernel(q_ref, gate_ref, kc_ref, vc_ref, ovt_ref, kw_ref, vw_ref, ocw_ref, bias_ref,
                    *, tq, tw, rep, n_pick):
    i = pl.program_id(1)
    t0 = i * tq
    rows = rep * tq
    q3 = q_ref[...].reshape(rows, HEAD_DIM)
    gates = gate_ref[...]

    kc = kc_ref[0]
    n_cmp = kc.shape[0]
    cmp_end = lax.broadcasted_iota(jnp.int32, (tq, n_cmp), 1) * CMP_STRIDE + (CMP_BLOCK - 1)
    t_c = t0 + lax.broadcasted_iota(jnp.int32, (tq, n_cmp), 0)
    bias_c = jnp.where(cmp_end <= t_c, 0.0, NEG_BIG)
    sees_any = t0 + lax.broadcasted_iota(jnp.int32, (tq, 1), 0) >= CMP_BLOCK - 1
    o_c, inv_c, p_sum = [], [], None
    for r in range(rep):
        e_r, inv_r = _softmax2_parts(_dot_nt(q_ref[r], kc), bias_c)
        inv_r = jnp.where(sees_any, inv_r, 0.0)
        o_c.append(_dot(e_r.astype(BF16), vc_ref[0]))
        inv_c.append(inv_r)
        p_r = e_r * inv_r
        p_sum = p_r if p_sum is None else p_sum + p_r

    p_hi = p_sum.astype(BF16)
    p_lo = (p_sum - p_hi.astype(F32)).astype(BF16)
    imp_all = _dot_nt(ovt_ref[...], p_hi) + _dot_nt(ovt_ref[...], p_lo)
    n_slc = imp_all.shape[0]
    blk = lax.broadcasted_iota(jnp.int32, (n_slc, LANES), 0)
    blk_f = blk.astype(F32)
    for c in range(tq // LANES):
        imp = imp_all[:, c * LANES:(c + 1) * LANES]
        jt = (t0 + c * LANES + lax.broadcasted_iota(jnp.int32, imp.shape, 1)) // SLC_BLOCK
        valid = blk <= jt
        forced = (blk == 0) | (blk == jt) | (blk == jt - 1)
        work = jnp.where(valid, jnp.where(forced, -jnp.inf, imp), -jnp.inf)
        sel_bias = jnp.where(forced, 0.0, NEG_BIG)
        for _ in range(n_pick):
            mx = jnp.max(work, axis=0, keepdims=True)
            first = jnp.min(jnp.where(work == mx, blk_f, float(n_slc)), axis=0, keepdims=True)
            pick = blk_f == first
            sel_bias = jnp.where(pick, 0.0, sel_bias)
            work = jnp.where(pick, -jnp.inf, work)
        bias_ref[0, c * LANES:(c + 1) * LANES, :] = jnp.where(valid, sel_bias, NEG_BIG).T.astype(bias_ref.dtype)

    wlen = tw + WINDOW
    rel = lax.broadcasted_iota(jnp.int32, (tw, wlen), 1) - lax.broadcasted_iota(jnp.int32, (tw, wlen), 0)
    for sub in range(tq // tw):
        ts0 = t0 + sub * tw
        ws = pl.multiple_of(jnp.maximum(ts0 - WINDOW, 0), tw)
        qs = jnp.concatenate([q_ref[r, sub * tw:(sub + 1) * tw, :] for r in range(rep)], axis=0)
        s_w = _dot_nt(qs, kw_ref[0, pl.ds(ws, wlen), :])
        rel_w = rel + (ws - ts0)
        bias_w = jnp.where((rel_w <= 0) & (rel_w > -WINDOW), 0.0, NEG_BIG)
        e_w, inv_w = _softmax2_parts(s_w, jnp.concatenate([bias_w] * rep, axis=0))
        o_w = _dot(e_w.astype(BF16), vw_ref[0, pl.ds(ws, wlen), :])
        gs = gates[sub * tw:(sub + 1) * tw]
        for r in range(rep):
            cr = slice(sub * tw, (sub + 1) * tw)
            wr = slice(r * tw, (r + 1) * tw)
            ocw_ref[cr, r * HEAD_DIM:(r + 1) * HEAD_DIM] = (
                (gs[:, 3 * r:3 * r + 1] * inv_c[r][cr]) * o_c[r][cr]
                + (gs[:, 3 * r + 2:3 * r + 3] * inv_w[wr]) * o_w[wr])


def _nsa_cmp(q, gates, cmp_kv, overlap, kw, vw, *, tq=512, tw=128):
    n_h, s, _ = q.shape
    g = NSA_KV_GROUPS
    rep = n_h // g
    nc = cmp_kv.shape[1]
    n_slc = s // SLC_BLOCK
    tq = min(tq, s)
    n_pick = max(min(N_SELECT, n_slc) - N_FORCED, 0)
    kernel = functools.partial(_nsa_cmp_kernel, tq=tq, tw=tw, rep=rep, n_pick=n_pick)
    return pl.pallas_call(
        kernel,
        out_shape=(jax.ShapeDtypeStruct((s, n_h * HEAD_DIM), F32), jax.ShapeDtypeStruct((g, s, n_slc), BF16)),
        grid=(g, s // tq),
        in_specs=[
            pl.BlockSpec((rep, tq, HEAD_DIM), lambda gi, i: (gi, i, 0)),
            pl.BlockSpec((tq, LANES), lambda gi, i: (i, gi)),
            pl.BlockSpec((1, nc, HEAD_DIM), lambda gi, i: (gi, 0, 0)),
            pl.BlockSpec((1, nc, HEAD_DIM), lambda gi, i: (g + gi, 0, 0)),
            pl.BlockSpec((n_slc, nc), lambda gi, i: (0, 0)),
            pl.BlockSpec((1, s, HEAD_DIM), lambda gi, i: (gi, 0, 0)),
            pl.BlockSpec((1, s, HEAD_DIM), lambda gi, i: (gi, 0, 0)),
        ],
        out_specs=(pl.BlockSpec((tq, rep * HEAD_DIM), lambda gi, i: (i, gi)),
                   pl.BlockSpec((1, tq, n_slc), lambda gi, i: (gi, i, 0))),
        compiler_params=_params(("arbitrary", "arbitrary")),
        name="nsa_cmp_win",
    )(q, gates, cmp_kv, cmp_kv, overlap, kw, vw)


def _nsa_sel_kernel(q_ref, bias_ref, ks_ref, vs_ref, ocw_ref, gate_ref, o_ref, m_ref, acc_ref, sa_ref, sb_ref,
                    *, tq, tk, rep):
    i = pl.program_id(1)
    t0 = i * tq
    rows = rep * tq
    q3 = q_ref[...].reshape(rows, HEAD_DIM)
    q_aug = jnp.concatenate([q3, jnp.concatenate([bias_ref[0]] * rep, axis=0)], axis=1)

    m_ref[...] = jnp.full(m_ref.shape, NEG_BIG, F32)
    acc_ref[...] = jnp.zeros(acc_ref.shape, F32)

    def scores(kt, s_ref):
        k_aug = ks_ref[0, pl.ds(pl.multiple_of(kt * tk, tk), tk), :]
        for r in range(rep):
            rs = slice(r * tq, (r + 1) * tq)
            s_ref[rs, :] = _dot_nt(q_aug[rs], k_aug)

    def consume(kt, s_ref, causal):
        k0 = pl.multiple_of(kt * tk, tk)
        v_aug = vs_ref[0, pl.ds(k0, tk), :]
        for r in range(rep):
            rs = slice(r * tq, (r + 1) * tq)
            s = s_ref[rs, :]
            if causal:
                t_s = t0 + lax.broadcasted_iota(jnp.int32, s.shape, 0)
                s = jnp.where(lax.broadcasted_iota(jnp.int32, s.shape, 1) + k0 <= t_s, s, NEG_BIG)
            m_prev = m_ref[rs, :]
            m_new = jnp.maximum(m_prev, jnp.max(s, axis=-1, keepdims=True))
            alpha = jnp.exp2(m_prev - m_new)
            p = jnp.exp2(s - jnp.concatenate([m_new] * (tk // LANES), axis=1))
            acc_ref[rs, :] = (jnp.concatenate([alpha, alpha], axis=1) * acc_ref[rs, :]
                              + _dot(p.astype(BF16), v_aug))
            m_ref[rs, :] = m_new

    last = (t0 + tq + tk - 1) // tk - 1
    n_pairs = last // 2

    scores(0, sa_ref)

    def pair(a, carry):
        kt = 2 * a
        scores(kt + 1, sb_ref)
        consume(kt, sa_ref, False)
        scores(kt + 2, sa_ref)
        consume(kt + 1, sb_ref, False)
        return carry

    lax.fori_loop(0, n_pairs, pair, 0)

    @pl.when(last == 2 * n_pairs)
    def _():
        consume(last, sa_ref, True)

    @pl.when(last != 2 * n_pairs)
    def _():
        scores(last, sb_ref)
        consume(last - 1, sa_ref, False)
        consume(last, sb_ref, True)

    acc = acc_ref[...]
    o_s = acc[:, :HEAD_DIM] / acc[:, HEAD_DIM:]
    gates = gate_ref[...]
    for r in range(rep):
        cs = slice(r * HEAD_DIM, (r + 1) * HEAD_DIM)
        o_ref[:, cs] = (ocw_ref[:, cs] + gates[:, 3 * r + 1:3 * r + 2] * o_s[r * tq:(r + 1) * tq]).astype(o_ref.dtype)


def _nsa_sel(q, gates, sel_bias, ks_aug, vs_aug, ocw, *, tq=512, tk=512):
    n_h, s, _ = q.shape
    g = NSA_KV_GROUPS
    rep = n_h // g
    n_slc = sel_bias.shape[2]
    tk = min(tk, s)
    tq = min(tq, tk)
    rows = rep * tq
    kernel = functools.partial(_nsa_sel_kernel, tq=tq, tk=tk, rep=rep)
    return pl.pallas_call(
        kernel,
        out_shape=jax.ShapeDtypeStruct((s, n_h * HEAD_DIM), BF16),
        grid=(g, s // tq),
        in_specs=[
            pl.BlockSpec((rep, tq, HEAD_DIM), lambda gi, i: (gi, i, 0)),
            pl.BlockSpec((1, tq, n_slc), lambda gi, i: (gi, i, 0)),
            pl.BlockSpec((1, s, HEAD_DIM + n_slc), lambda gi, i: (gi, 0, 0)),
            pl.BlockSpec((1, s, 2 * HEAD_DIM), lambda gi, i: (gi, 0, 0)),
            pl.BlockSpec((tq, rep * HEAD_DIM), lambda gi, i: (i, gi)),
            pl.BlockSpec((tq, LANES), lambda gi, i: (i, gi)),
        ],
        out_specs=pl.BlockSpec((tq, rep * HEAD_DIM), lambda gi, i: (i, gi)),
        scratch_shapes=[pltpu.VMEM((rows, LANES), F32), pltpu.VMEM((rows, 2 * HEAD_DIM), F32),
                        pltpu.VMEM((rows, tk), F32), pltpu.VMEM((rows, tk), F32)],
        compiler_params=_params(("arbitrary", "arbitrary")),
        name="nsa_sel",
    )(q, sel_bias, ks_aug, vs_aug, ocw, gates)


def _slc_from_cmp(n_slc, n_cmp):
    cs = jnp.arange(n_cmp)[None, :] * CMP_STRIDE
    ss = jnp.arange(n_slc)[:, None] * SLC_BLOCK
    ov = jnp.clip(jnp.minimum(cs + CMP_BLOCK, ss + SLC_BLOCK) - jnp.maximum(cs, ss), 0, None)
    return (ov.astype(F32) / CMP_BLOCK).astype(BF16)


def kernel(x, mem, positions, ffn_norm, ffn_w_gate, ffn_w_up, ffn_w_down, mix_norm, mem_norm, w_mem_kv, w_out, w_in_conv, conv_w, w_in_nsa, kv_norm, w_kv, cmp_pos_k, cmp_w1_k, cmp_w2_k, cmp_pos_v, cmp_w1_v, cmp_w2_v, final_norm):
    b, s, d = x.shape
    assert b == 1
    depth = ffn_norm.shape[0]
    n_a = w_in_conv.shape[0]
    mem_dim = w_mem_kv.shape[2] // 2
    conv_ch = conv_w.shape[2]
    kv_w = w_kv.shape[1] // 6
    g = NSA_KV_GROUPS
    n_q = w_in_nsa.shape[2] - mem_dim
    n_heads = n_q // (HEAD_DIM + 3)
    q_dim = n_heads * HEAD_DIM
    rep = n_heads // g
    n_slc = s // SLC_BLOCK
    bf = lambda w: w.astype(BF16)

    h = x[0]
    mem2 = mem[0]
    tables = _rope_tables(positions.reshape(s, 1))

    kv_side = None
    for layer in range(depth):
        if layer == n_a:
            chunks, ks_aug, vs_aug, kw, vw = _kv_side(h, kv_norm, w_kv, tables)
            nc = s // CMP_STRIDE
            pos = bf(jnp.stack([cmp_pos_k, cmp_pos_v]).reshape(2, 1, CMP_BLOCK * HEAD_DIM))
            cmp_kv = _compress(chunks, pos, bf(jnp.stack([cmp_w1_k, cmp_w1_v])), bf(jnp.stack([cmp_w2_k, cmp_w2_v])))
            kv_side = (cmp_kv, ks_aug, vs_aug, kw, vw, _slc_from_cmp(n_slc, nc))

        h = _ffn(h, ffn_norm[layer, 0], ffn_w_gate, ffn_w_up, ffn_w_down, layer, 0)

        if layer < n_a:
            tok, q_mem = _mix_a(h, mix_norm[layer], w_in_conv, conv_w, layer)
        else:
            w_in = w_in_nsa[layer - n_a]
            wg = w_in[:, q_dim:q_dim + 3 * n_heads].reshape(d, g, 3 * rep)
            wg = jnp.pad(wg, ((0, 0), (0, 0), (0, LANES - 3 * rep))).reshape(d, g * LANES)
            w_cat = bf(jnp.concatenate([w_in[:, :q_dim], w_in[:, q_dim + 3 * n_heads:], wg], axis=1))
            q, q_mem, gates = _mix_b(h, mix_norm[layer], w_cat, tables, q_dim=q_dim, mem_dim=mem_dim,
                                     scale=HEAD_DIM ** -0.5 * math.log2(math.e))
            cmp_kv, ks_aug, vs_aug, kw, vw, overlap = kv_side
            ocw, sel_bias = _nsa_cmp(q, gates, cmp_kv, overlap, kw, vw)
            tok = _nsa_sel(q, gates, sel_bias, ks_aug, vs_aug, ocw)

        mem_o = _mem_attn(q_mem, _mem_kv(mem2, mem_norm[layer], w_mem_kv, layer))
        h = _out_proj(h, tok, mem_o, w_out, layer)

        last = layer == depth - 1
        h = _ffn(h, ffn_norm[layer, 1], ffn_w_gate, ffn_w_up, ffn_w_down, layer, 1, final_norm if last else None)

    return h[None]
```

```python
import functools
import math

import jax
import jax.numpy as jnp
from jax import lax
from jax.experimental import pallas as pl
from jax.experimental.pallas import tpu as pltpu

HEAD_DIM = 128
ROT_DIM = HEAD_DIM // 4
ROPE_THETA = 500000.0
MEM_HEADS = 4
NSA_KV_GROUPS = 4
CMP_STRIDE = 16
CMP_BLOCK = 2 * CMP_STRIDE
SLC_BLOCK = 64
N_SELECT = 16
N_FORCED = 3
WINDOW = 512
RMS_EPS = 1e-6
CONV_K = 3

LANES = 128
SUBLANES = 8
NEG_BIG = -1e30
VMEM_LIMIT = 56 * 1024 * 1024

F32 = jnp.float32
BF16 = jnp.bfloat16


def _dot(a, b):
    return jnp.dot(a, b, preferred_element_type=F32)


def _dot_nt(a, b):
    return lax.dot_general(a, b, (((1,), (1,)), ((), ())), preferred_element_type=F32)


def _rms(x, g):
    ms = jnp.mean(x * x, axis=-1, keepdims=True)
    return x * lax.rsqrt(ms + RMS_EPS) * g


def _params(sem):
    return pltpu.CompilerParams(dimension_semantics=sem, vmem_limit_bytes=VMEM_LIMIT)


def _rope_table_kernel(pos_ref, inv_ref, cos_ref, sin_ref):
    ang = pos_ref[...].astype(F32) * inv_ref[...]
    lane = lax.broadcasted_iota(jnp.int32, ang.shape, 1)
    c = jnp.cos(ang)
    s = jnp.sin(ang)
    half = ROT_DIM // 2
    cos_ref[...] = jnp.where(lane < ROT_DIM, c, 1.0)
    sin_ref[...] = jnp.where(lane < half, -s, jnp.where(lane < ROT_DIM, s, 0.0))


def _rope_tables(positions_col):
    s = positions_col.shape[0]
    half = ROT_DIM // 2
    inv = 1.0 / (ROPE_THETA ** (jnp.arange(half, dtype=F32) / half))
    inv_row = jnp.concatenate([inv, inv, jnp.zeros((LANES - ROT_DIM,), F32)])[None, :]
    tm = min(s, 1024)
    return pl.pallas_call(
        _rope_table_kernel,
        out_shape=(jax.ShapeDtypeStruct((s, LANES), F32), jax.ShapeDtypeStruct((s, LANES), F32)),
        grid=(s // tm,),
        in_specs=[pl.BlockSpec((tm, 1), lambda i: (i, 0)), pl.BlockSpec((1, LANES), lambda i: (0, 0))],
        out_specs=(pl.BlockSpec((tm, LANES), lambda i: (i, 0)), pl.BlockSpec((tm, LANES), lambda i: (i, 0))),
        compiler_params=_params(("arbitrary",)),
        name="rope_tables",
    )(positions_col, inv_row)


def _rotate(x, cos_t, sin_t):
    half = ROT_DIM // 2
    lane = lax.broadcasted_iota(jnp.int32, x.shape, 1)
    swapped = jnp.where(lane < half, pltpu.roll(x, LANES - half, axis=1), pltpu.roll(x, half, axis=1))
    return x * cos_t + swapped * sin_t


def _store_heads(o_ref, acc, tables):
    for h in range(acc.shape[1] // HEAD_DIM):
        xh = acc[:, h * HEAD_DIM:(h + 1) * HEAD_DIM]
        if tables is not None:
            xh = _rotate(xh, tables[0][...], tables[1][...])
        o_ref[h] = xh.astype(o_ref.dtype)


def _ffn_kernel(x_ref, g_ref, wg_ref, wu_ref, wd_ref, *rest, final):
    if final:
        fg_ref, o_ref, xn_ref = rest
    else:
        o_ref, xn_ref = rest
    j = pl.program_id(1)

    @pl.when(j == 0)
    def _():
        x = x_ref[...]
        xn_ref[...] = _rms(x, g_ref[...]).astype(BF16)
        o_ref[...] = x

    xn = xn_ref[...]
    a = _dot(xn, wg_ref[...].astype(BF16))
    b = _dot(xn, wu_ref[...].astype(BF16))
    mid = (a * jax.nn.sigmoid(a) * b).astype(BF16)
    o_ref[...] += 0.5 * _dot(mid, wd_ref[...].astype(BF16))

    if final:
        @pl.when(j == pl.num_programs(1) - 1)
        def _():
            o_ref[...] = _rms(o_ref[...], fg_ref[...])


def _ffn(x, g, w_gate, w_up, w_down, layer, half, final_g=None, *, tm=1024, tf=256):
    s, d = x.shape
    f = w_gate.shape[3]
    tm = min(tm, s)
    final = final_g is not None
    in_specs = [
        pl.BlockSpec((tm, d), lambda i, j: (i, 0)),
        pl.BlockSpec((1, d), lambda i, j: (0, 0)),
        pl.BlockSpec((None, None, d, tf), lambda i, j: (layer, half, 0, j)),
        pl.BlockSpec((None, None, d, tf), lambda i, j: (layer, half, 0, j)),
        pl.BlockSpec((None, None, tf, d), lambda i, j: (layer, half, j, 0)),
    ]
    args = [x, g[None, :], w_gate, w_up, w_down]
    if final:
        in_specs.append(pl.BlockSpec((1, d), lambda i, j: (0, 0)))
        args.append(final_g[None, :])
    return pl.pallas_call(
        functools.partial(_ffn_kernel, final=final),
        out_shape=jax.ShapeDtypeStruct((s, d), F32),
        grid=(s // tm, f // tf),
        in_specs=in_specs,
        out_specs=pl.BlockSpec((tm, d), lambda i, j: (i, 0)),
        scratch_shapes=[pltpu.VMEM((tm, d), BF16)],
        compiler_params=_params(("arbitrary", "arbitrary")),
        name="ffn_final" if final else "ffn",
    )(*args)


def _mem_kv_kernel(x_ref, g_ref, w_ref, o_ref):
    xn = _rms(x_ref[...], g_ref[...]).astype(BF16)
    o_ref[...] = _dot(xn, w_ref[...].astype(BF16)).astype(o_ref.dtype)


def _mem_kv(mem2, g, w_mem_kv, layer, *, tn=512):
    n_mem, d = mem2.shape
    n = w_mem_kv.shape[2]
    return pl.pallas_call(
        _mem_kv_kernel,
        out_shape=jax.ShapeDtypeStruct((n_mem, n), BF16),
        grid=(n // tn,),
        in_specs=[
            pl.BlockSpec((n_mem, d), lambda j: (0, 0)),
            pl.BlockSpec((1, d), lambda j: (0, 0)),
            pl.BlockSpec((None, d, tn), lambda j: (layer, 0, j)),
        ],
        out_specs=pl.BlockSpec((n_mem, tn), lambda j: (0, j)),
        compiler_params=_params(("arbitrary",)),
        name="mem_kv",
    )(mem2, g[None, :], w_mem_kv)


def _kv_side_kernel(x_ref, g_ref, w_ref, cos_ref, sin_ref, chunk_ref, ks_ref, vs_ref, kw_ref, vw_ref,
                    xn_ref, stage_ref, *, n_grp):
    i = pl.program_id(0)
    j = pl.program_id(1)
    tm = x_ref.shape[0]

    @pl.when(j == 0)
    def _():
        xn_ref[...] = _rms(x_ref[...], g_ref[...]).astype(BF16)

    acc = _dot(xn_ref[...], w_ref[...].astype(BF16))

    def head(h, rot):
        xh = acc[:, h * HEAD_DIM:(h + 1) * HEAD_DIM]
        return _rotate(xh, cos_ref[...], sin_ref[...]) if rot else xh

    def store_chunks(first, rot):
        for h in range(n_grp):
            stage_ref[...] = head(h, rot)
            for p in range(CMP_STRIDE):
                chunk_ref[first + h, :, p * HEAD_DIM:(p + 1) * HEAD_DIM] = (
                    stage_ref[pl.ds(p, tm // CMP_STRIDE, stride=CMP_STRIDE), :].astype(chunk_ref.dtype))

    @pl.when(j == 0)
    def _():
        store_chunks(0, True)

    @pl.when(j == 1)
    def _():
        store_chunks(n_grp, False)

    @pl.when(j == 2)
    def _():
        n_slc = ks_ref.shape[2] - HEAD_DIM
        row_blk = (i * tm + lax.broadcasted_iota(jnp.int32, (tm, n_slc), 0)) // SLC_BLOCK
        code = jnp.where(row_blk == lax.broadcasted_iota(jnp.int32, (tm, n_slc), 1), 1.0, 0.0).astype(ks_ref.dtype)
        for h in range(n_grp):
            ks_ref[h, :, 0:HEAD_DIM] = head(h, True).astype(ks_ref.dtype)
            ks_ref[h, :, HEAD_DIM:] = code

    @pl.when(j == 3)
    def _():
        for h in range(n_grp):
            vs_ref[h, :, 0:HEAD_DIM] = head(h, False).astype(vs_ref.dtype)
            vs_ref[h, :, HEAD_DIM:] = jnp.ones((tm, HEAD_DIM), vs_ref.dtype)

    @pl.when(j == 4)
    def _():
        for h in range(n_grp):
            kw_ref[h] = head(h, True).astype(kw_ref.dtype)

    @pl.when(j == 5)
    def _():
        for h in range(n_grp):
            vw_ref[h] = head(h, False).astype(vw_ref.dtype)


def _kv_side(x, g, w_kv, tables, *, tm=1024):
    s, d = x.shape
    grp = NSA_KV_GROUPS
    tn = grp * HEAD_DIM
    assert w_kv.shape[1] == 6 * tn
    tm = min(tm, s)
    n_slc = s // SLC_BLOCK
    nc = s // CMP_STRIDE
    cw = CMP_STRIDE * HEAD_DIM
    row = lambda i, j: (0, i, 0)
    return pl.pallas_call(
        functools.partial(_kv_side_kernel, n_grp=grp),
        out_shape=(jax.ShapeDtypeStruct((2 * grp, nc, cw), BF16),
                   jax.ShapeDtypeStruct((grp, s, HEAD_DIM + n_slc), BF16),
                   jax.ShapeDtypeStruct((grp, s, 2 * HEAD_DIM), BF16),
                   jax.ShapeDtypeStruct((grp, s, HEAD_DIM), BF16),
                   jax.ShapeDtypeStruct((grp, s, HEAD_DIM), BF16)),
        grid=(s // tm, 6),
        in_specs=[
            pl.BlockSpec((tm, d), lambda i, j: (i, 0)),
            pl.BlockSpec((1, d), lambda i, j: (0, 0)),
            pl.BlockSpec((d, tn), lambda i, j: (0, j)),
            pl.BlockSpec((tm, LANES), lambda i, j: (i, 0)),
            pl.BlockSpec((tm, LANES), lambda i, j: (i, 0)),
        ],
        out_specs=(pl.BlockSpec((2 * grp, tm // CMP_STRIDE, cw), row),
                   pl.BlockSpec((grp, tm, HEAD_DIM + n_slc), row),
                   pl.BlockSpec((grp, tm, 2 * HEAD_DIM), row),
                   pl.BlockSpec((grp, tm, HEAD_DIM), row),
                   pl.BlockSpec((grp, tm, HEAD_DIM), row)),
        scratch_shapes=[pltpu.VMEM((tm, d), BF16), pltpu.VMEM((tm, HEAD_DIM), F32)],
        compiler_params=_params(("arbitrary", "arbitrary")),
        name="kv_side",
    )(x, g[None, :], w_kv, *tables)


def _mix_a_kernel(x_ref, g_ref, wb_ref, wc_ref, wv_ref, wq_ref, cw_ref, tok_ref, qmem_ref,
                  xn_ref, halo_ref, ext_ref):
    i = pl.program_id(0)
    j = pl.program_id(1)
    tm = x_ref.shape[0]
    tc = tok_ref.shape[1]

    @pl.when(j == 0)
    def _():
        xn_ref[...] = _rms(x_ref[...], g_ref[...]).astype(BF16)
        qmem_ref[...] = _dot(xn_ref[...], wq_ref[...].astype(BF16)).astype(qmem_ref.dtype)

    xn = xn_ref[...]
    gate_b = _dot(xn, wb_ref[...].astype(BF16))
    p = _dot(xn, wc_ref[...].astype(BF16)) * _dot(xn, wv_ref[...].astype(BF16))

    @pl.when(i == 0)
    def _():
        halo_ref[j] = jnp.zeros((SUBLANES, tc), F32)

    ext_ref[0:SUBLANES, :] = halo_ref[j]
    ext_ref[SUBLANES:, :] = p
    halo_ref[j] = p[tm - SUBLANES:, :]
    y = cw_ref[CONV_K - 1:CONV_K, :] * p
    for k in range(CONV_K - 1):
        off = SUBLANES - (CONV_K - 1 - k)
        y = y + cw_ref[k:k + 1, :] * ext_ref[off:off + tm, :]
    tok_ref[...] = (gate_b * y).astype(tok_ref.dtype)


def _mix_a(x, g, w_in_conv, conv_w, layer, *, tm=1024, tc=256):
    s, d = x.shape
    n_ch = conv_w.shape[2]
    nt = n_ch // tc
    md = w_in_conv.shape[2] - 3 * n_ch
    assert (3 * n_ch) % md == 0
    tm = min(tm, s)
    w_tile = lambda part: pl.BlockSpec((None, d, tc), lambda i, j: (layer, 0, part * nt + j))
    return pl.pallas_call(
        _mix_a_kernel,
        out_shape=(jax.ShapeDtypeStruct((s, n_ch), BF16), jax.ShapeDtypeStruct((s, md), BF16)),
        grid=(s // tm, nt),
        in_specs=[
            pl.BlockSpec((tm, d), lambda i, j: (i, 0)),
            pl.BlockSpec((1, d), lambda i, j: (0, 0)),
            w_tile(0), w_tile(1), w_tile(2),
            pl.BlockSpec((None, d, md), lambda i, j: (layer, 0, 3 * n_ch // md), pipeline_mode=pl.Buffered(1)),
            pl.BlockSpec((None, CONV_K, tc), lambda i, j: (layer, 0, j)),
        ],
        out_specs=(pl.BlockSpec((tm, tc), lambda i, j: (i, j)), pl.BlockSpec((tm, md), lambda i, j: (i, 0))),
        scratch_shapes=[pltpu.VMEM((tm, d), BF16), pltpu.VMEM((nt, SUBLANES, tc), F32),
                        pltpu.VMEM((tm + SUBLANES, tc), F32)],
        compiler_params=_params(("arbitrary", "arbitrary")),
        name="mix_a_in",
    )(x, g[None, :], w_in_conv, w_in_conv, w_in_conv, w_in_conv, conv_w)


def _mix_b_kernel(x_ref, g_ref, w_ref, cos_ref, sin_ref, q_ref, qmem_ref, gate_ref, xn_ref, *, q_tiles, m_tiles, scale):
    j = pl.program_id(1)

    @pl.when(j == 0)
    def _():
        xn_ref[...] = _rms(x_ref[...], g_ref[...]).astype(BF16)

    acc = _dot(xn_ref[...], w_ref[...])

    @pl.when(j < q_tiles)
    def _():
        _store_heads(q_ref, acc * scale, (cos_ref, sin_ref))

    @pl.when((j >= q_tiles) & (j < q_tiles + m_tiles))
    def _():
        qmem_ref[...] = acc.astype(qmem_ref.dtype)

    @pl.when(j >= q_tiles + m_tiles)
    def _():
        gate_ref[...] = jax.nn.sigmoid(acc)


def _mix_b(x, g, w, tables, *, q_dim, mem_dim, scale, tm=1024, tn=512):
    s, d = x.shape
    n = w.shape[1]
    q_tiles = q_dim // tn
    m_tiles = mem_dim // tn
    hp = tn // HEAD_DIM
    return pl.pallas_call(
        functools.partial(_mix_b_kernel, q_tiles=q_tiles, m_tiles=m_tiles, scale=scale),
        out_shape=(jax.ShapeDtypeStruct((q_dim // HEAD_DIM, s, HEAD_DIM), BF16),
                   jax.ShapeDtypeStruct((s, mem_dim), BF16),
                   jax.ShapeDtypeStruct((s, n - q_dim - mem_dim), F32)),
        grid=(s // tm, n // tn),
        in_specs=[
            pl.BlockSpec((tm, d), lambda i, j: (i, 0)),
            pl.BlockSpec((1, d), lambda i, j: (0, 0)),
            pl.BlockSpec((d, tn), lambda i, j: (0, j)),
            pl.BlockSpec((tm, LANES), lambda i, j: (i, 0)),
            pl.BlockSpec((tm, LANES), lambda i, j: (i, 0)),
        ],
        out_specs=(
            pl.BlockSpec((hp, tm, HEAD_DIM), lambda i, j: (jnp.minimum(j, q_tiles - 1), i, 0)),
            pl.BlockSpec((tm, tn), lambda i, j: (i, jnp.clip(j - q_tiles, 0, m_tiles - 1))),
            pl.BlockSpec((tm, tn), lambda i, j: (i, jnp.maximum(j - q_tiles - m_tiles, 0))),
        ),
        scratch_shapes=[pltpu.VMEM((tm, d), BF16)],
        compiler_params=_params(("arbitrary", "arbitrary")),
        name="mix_b_in",
    )(x, g[None, :], w, *tables)


def _mix_out_kernel(h_ref, tok_ref, qmem_ref, k_ref, v_ref, w1_ref, w2_ref, o_ref, memo_ref):
    @pl.when(pl.program_id(1) == 0)
    def _():
        scale = HEAD_DIM ** -0.5
        for h in range(MEM_HEADS):
            sl = slice(h * HEAD_DIM, (h + 1) * HEAD_DIM)
            s = _dot_nt(qmem_ref[:, sl], k_ref[:, sl]) * scale
            e = jnp.exp(s - jnp.max(s, axis=-1, keepdims=True))
            o = _dot(e.astype(BF16), v_ref[:, sl]) * (1.0 / jnp.sum(e, axis=-1, keepdims=True))
            memo_ref[:, sl] = o.astype(memo_ref.dtype)

    o_ref[...] = (h_ref[...] + _dot(tok_ref[...], w1_ref[...].astype(BF16))
                  + _dot(memo_ref[...], w2_ref[...].astype(BF16)))


def _mix_out(h, tok, q_mem, mem_kv, w_out, layer, *, tm=2048, tn=512):
    s, d = h.shape
    n_tok = tok.shape[1]
    md = q_mem.shape[1]
    n_mem = mem_kv.shape[0]
    assert n_tok % md == 0
    tm = min(tm, s)
    return pl.pallas_call(
        _mix_out_kernel,
        out_shape=jax.ShapeDtypeStruct((s, d), F32),
        grid=(s // tm, d // tn),
        in_specs=[
            pl.BlockSpec((tm, tn), lambda i, j: (i, j)),
            pl.BlockSpec((tm, n_tok), lambda i, j: (i, 0)),
            pl.BlockSpec((tm, md), lambda i, j: (i, 0)),
            pl.BlockSpec((n_mem, md), lambda i, j: (0, 0)),
            pl.BlockSpec((n_mem, md), lambda i, j: (0, 1)),
            pl.BlockSpec((None, n_tok, tn), lambda i, j: (layer, 0, j)),
            pl.BlockSpec((None, md, tn), lambda i, j: (layer, n_tok // md, j)),
        ],
        out_specs=pl.BlockSpec((tm, tn), lambda i, j: (i, j)),
        scratch_shapes=[pltpu.VMEM((tm, md), BF16)],
        compiler_params=_params(("arbitrary", "arbitrary")),
        name="mix_out",
    )(h, tok, q_mem, mem_kv, mem_kv, w_out, w_out)


def _compress_kernel(c_ref, pos_ref, w1_ref, w2_ref, o_ref):
    x = c_ref[0]
    nc, half = x.shape
    w1a = w1_ref[0, 0:half, :]
    w1b = w1_ref[0, half:, :]
    pos = jnp.broadcast_to(pos_ref[0], (SUBLANES, 2 * half))
    bias = _dot(pos, w1_ref[0])[0:1, :]
    first = _dot(x, w1a)
    second = pltpu.roll(_dot(x, w1b), nc - 1, axis=0)
    hid = jax.nn.gelu(first + second + bias, approximate=True)
    out = _dot(hid.astype(BF16), w2_ref[0])
    row = lax.broadcasted_iota(jnp.int32, out.shape, 0)
    o_ref[0] = jnp.where(row < nc - 1, out, 0.0).astype(o_ref.dtype)


def _compress(chunks, pos, w1, w2):
    n, nc, cw = chunks.shape
    g = n // 2
    return pl.pallas_call(
        _compress_kernel,
        out_shape=jax.ShapeDtypeStruct((2 * g, nc, HEAD_DIM), BF16),
        grid=(2 * g,),
        in_specs=[
            pl.BlockSpec((1, nc, cw), lambda i: (i, 0, 0)),
            pl.BlockSpec((1, 1, 2 * cw), lambda i: (i // g, 0, 0)),
            pl.BlockSpec((1, 2 * cw, HEAD_DIM), lambda i: (i // g, 0, 0)),
            pl.BlockSpec((1, HEAD_DIM, HEAD_DIM), lambda i: (i // g, 0, 0)),
        ],
        out_specs=pl.BlockSpec((1, nc, HEAD_DIM), lambda i: (i, 0, 0)),
        compiler_params=_params(("arbitrary",)),
        name="compress",
    )(chunks, pos, w1, w2)


def _softmax2_parts(s, bias):
    s = s + bias
    e = jnp.exp2(s - jnp.max(s, axis=-1, keepdims=True))
    return e, 1.0 / jnp.sum(e, axis=-1, keepdims=True)


def _nsa_cmp_kernel(q_ref, gate_ref, kc_ref, vc_ref, ovt_ref, kw_ref, vw_ref, ocw_ref, bias_ref,
                    *, tq, tw, rep, n_pick):
    i = pl.program_id(1)
    t0 = i * tq
    rows = rep * tq
    q3 = q_ref[...].reshape(rows, HEAD_DIM)
    gates = gate_ref[...]

    kc = kc_ref[0]
    n_cmp = kc.shape[0]
    cmp_end = lax.broadcasted_iota(jnp.int32, (tq, n_cmp), 1) * CMP_STRIDE + (CMP_BLOCK - 1)
    t_c = t0 + lax.broadcasted_iota(jnp.int32, (tq, n_cmp), 0)
    bias_c = jnp.where(cmp_end <= t_c, 0.0, NEG_BIG)
    sees_any = t0 + lax.broadcasted_iota(jnp.int32, (tq, 1), 0) >= CMP_BLOCK - 1
    o_c, inv_c, p_sum = [], [], None
    for r in range(rep):
        e_r, inv_r = _softmax2_parts(_dot_nt(q_ref[r], kc), bias_c)
        inv_r = jnp.where(sees_any, inv_r, 0.0)
        o_c.append(_dot(e_r.astype(BF16), vc_ref[0]))
        inv_c.append(inv_r)
        p_r = e_r * inv_r
        p_sum = p_r if p_sum is None else p_sum + p_r

    p_hi = p_sum.astype(BF16)
    p_lo = (p_sum - p_hi.astype(F32)).astype(BF16)
    imp_all = _dot_nt(ovt_ref[...], p_hi) + _dot_nt(ovt_ref[...], p_lo)
    n_slc = imp_all.shape[0]
    blk = lax.broadcasted_iota(jnp.int32, (n_slc, LANES), 0)
    blk_f = blk.astype(F32)
    for c in range(tq // LANES):
        imp = imp_all[:, c * LANES:(c + 1) * LANES]
        jt = (t0 + c * LANES + lax.broadcasted_iota(jnp.int32, imp.shape, 1)) // SLC_BLOCK
        valid = blk <= jt
        forced = (blk == 0) | (blk == jt) | (blk == jt - 1)
        work = jnp.where(valid, jnp.where(forced, -jnp.inf, imp), -jnp.inf)
        sel_bias = jnp.where(forced, 0.0, NEG_BIG)
        for _ in range(n_pick):
            mx = jnp.max(work, axis=0, keepdims=True)
            first = jnp.min(jnp.where(work == mx, blk_f, float(n_slc)), axis=0, keepdims=True)
            pick = blk_f == first
            sel_bias = jnp.where(pick, 0.0, sel_bias)
            work = jnp.where(pick, -jnp.inf, work)
        bias_ref[0, c * LANES:(c + 1) * LANES, :] = jnp.where(valid, sel_bias, NEG_BIG).T.astype(bias_ref.dtype)

    wlen = tw + WINDOW
    rel = lax.broadcasted_iota(jnp.int32, (tw, wlen), 1) - lax.broadcasted_iota(jnp.int32, (tw, wlen), 0)
    for sub in range(tq // tw):
        ts0 = t0 + sub * tw
        ws = pl.multiple_of(jnp.maximum(ts0 - WINDOW, 0), tw)
        qs = jnp.concatenate([q_ref[r, sub * tw:(sub + 1) * tw, :] for r in range(rep)], axis=0)
        s_w = _dot_nt(qs, kw_ref[0, pl.ds(ws, wlen), :])
        rel_w = rel + (ws - ts0)
        bias_w = jnp.where((rel_w <= 0) & (rel_w > -WINDOW), 0.0, NEG_BIG)
        e_w, inv_w = _softmax2_parts(s_w, jnp.concatenate([bias_w] * rep, axis=0))
        o_w = _dot(e_w.astype(BF16), vw_ref[0, pl.ds(ws, wlen), :])
        gs = gates[sub * tw:(sub + 1) * tw]
        for r in range(rep):
            cr = slice(sub * tw, (sub + 1) * tw)
            wr = slice(r * tw, (r + 1) * tw)
            ocw_ref[cr, r * HEAD_DIM:(r + 1) * HEAD_DIM] = (
                (gs[:, 3 * r:3 * r + 1] * inv_c[r][cr]) * o_c[r][cr]
                + (gs[:, 3 * r + 2:3 * r + 3] * inv_w[wr]) * o_w[wr])


def _nsa_cmp(q, gates, cmp_kv, overlap, kw, vw, *, tq=512, tw=128):
    n_h, s, _ = q.shape
    g = NSA_KV_GROUPS
    rep = n_h // g
    nc = cmp_kv.shape[1]
    n_slc = s // SLC_BLOCK
    tq = min(tq, s)
    n_pick = max(min(N_SELECT, n_slc) - N_FORCED, 0)
    kernel = functools.partial(_nsa_cmp_kernel, tq=tq, tw=tw, rep=rep, n_pick=n_pick)
    return pl.pallas_call(
        kernel,
        out_shape=(jax.ShapeDtypeStruct((s, n_h * HEAD_DIM), F32), jax.ShapeDtypeStruct((g, s, n_slc), BF16)),
        grid=(g, s // tq),
        in_specs=[
            pl.BlockSpec((rep, tq, HEAD_DIM), lambda gi, i: (gi, i, 0)),
            pl.BlockSpec((tq, LANES), lambda gi, i: (i, gi)),
            pl.BlockSpec((1, nc, HEAD_DIM), lambda gi, i: (gi, 0, 0)),
            pl.BlockSpec((1, nc, HEAD_DIM), lambda gi, i: (g + gi, 0, 0)),
            pl.BlockSpec((n_slc, nc), lambda gi, i: (0, 0)),
            pl.BlockSpec((1, s, HEAD_DIM), lambda gi, i: (gi, 0, 0)),
            pl.BlockSpec((1, s, HEAD_DIM), lambda gi, i: (gi, 0, 0)),
        ],
        out_specs=(pl.BlockSpec((tq, rep * HEAD_DIM), lambda gi, i: (i, gi)),
                   pl.BlockSpec((1, tq, n_slc), lambda gi, i: (gi, i, 0))),
        compiler_params=_params(("arbitrary", "arbitrary")),
        name="nsa_cmp_win",
    )(q, gates, cmp_kv, cmp_kv, overlap, kw, vw)


def _nsa_sel_kernel(q_ref, bias_ref, ks_ref, vs_ref, ocw_ref, gate_ref, o_ref, m_ref, acc_ref, sa_ref, sb_ref,
                    *, tq, tk, rep):
    i = pl.program_id(1)
    t0 = i * tq
    rows = rep * tq
    q3 = q_ref[...].reshape(rows, HEAD_DIM)
    q_aug = jnp.concatenate([q3, jnp.concatenate([bias_ref[0]] * rep, axis=0)], axis=1)

    m_ref[...] = jnp.full(m_ref.shape, NEG_BIG, F32)
    acc_ref[...] = jnp.zeros(acc_ref.shape, F32)

    def scores(kt, s_ref):
        k_aug = ks_ref[0, pl.ds(pl.multiple_of(kt * tk, tk), tk), :]
        for r in range(rep):
            rs = slice(r * tq, (r + 1) * tq)
            s_ref[rs, :] = _dot_nt(q_aug[rs], k_aug)

    def consume(kt, s_ref, causal):
        k0 = pl.multiple_of(kt * tk, tk)
        v_aug = vs_ref[0, pl.ds(k0, tk), :]
        for r in range(rep):
            rs = slice(r * tq, (r + 1) * tq)
            s = s_ref[rs, :]
            if causal:
                t_s = t0 + lax.broadcasted_iota(jnp.int32, s.shape, 0)
                s = jnp.where(lax.broadcasted_iota(jnp.int32, s.shape, 1) + k0 <= t_s, s, NEG_BIG)
            m_prev = m_ref[rs, :]
            m_new = jnp.maximum(m_prev, jnp.max(s, axis=-1, keepdims=True))
            alpha = jnp.exp2(m_prev - m_new)
            p = jnp.exp2(s - jnp.concatenate([m_new] * (tk // LANES), axis=1))
            acc_ref[rs, :] = (jnp.concatenate([alpha, alpha], axis=1) * acc_ref[rs, :]
                              + _dot(p.astype(BF16), v_aug))
            m_ref[rs, :] = m_new

    last = (t0 + tq + tk - 1) // tk - 1
    n_pairs = last // 2

    scores(0, sa_ref)

    def pair(a, carry):
        kt = 2 * a
        scores(kt + 1, sb_ref)
        consume(kt, sa_ref, False)
        scores(kt + 2, sa_ref)
        consume(kt + 1, sb_ref, False)
        return carry

    lax.fori_loop(0, n_pairs, pair, 0)

    @pl.when(last == 2 * n_pairs)
    def _():
        consume(last, sa_ref, True)

    @pl.when(last != 2 * n_pairs)
    def _():
        scores(last, sb_ref)
        consume(last - 1, sa_ref, False)
        consume(last, sb_ref, True)

    acc = acc_ref[...]
    o_s = acc[:, :HEAD_DIM] / acc[:, HEAD_DIM:]
    gates = gate_ref[...]
    for r in range(rep):
        cs = slice(r * HEAD_DIM, (r + 1) * HEAD_DIM)
        o_ref[:, cs] = (ocw_ref[:, cs] + gates[:, 3 * r + 1:3 * r + 2] * o_s[r * tq:(r + 1) * tq]).astype(o_ref.dtype)


def _nsa_sel(q, gates, sel_bias, ks_aug, vs_aug, ocw, *, tq=512, tk=512):
    n_h, s, _ = q.shape
    g = NSA_KV_GROUPS
    rep = n_h // g
    n_slc = sel_bias.shape[2]
    tk = min(tk, s)
    tq = min(tq, tk)
    rows = rep * tq
    kernel = functools.partial(_nsa_sel_kernel, tq=tq, tk=tk, rep=rep)
    return pl.pallas_call(
        kernel,
        out_shape=jax.ShapeDtypeStruct((s, n_h * HEAD_DIM), BF16),
        grid=(g, s // tq),
        in_specs=[
            pl.BlockSpec((rep, tq, HEAD_DIM), lambda gi, i: (gi, i, 0)),
            pl.BlockSpec((1, tq, n_slc), lambda gi, i: (gi, i, 0)),
            pl.BlockSpec((1, s, HEAD_DIM + n_slc), lambda gi, i: (gi, 0, 0)),
            pl.BlockSpec((1, s, 2 * HEAD_DIM), lambda gi, i: (gi, 0, 0)),
            pl.BlockSpec((tq, rep * HEAD_DIM), lambda gi, i: (i, gi)),
            pl.BlockSpec((tq, LANES), lambda gi, i: (i, gi)),
        ],
        out_specs=pl.BlockSpec((tq, rep * HEAD_DIM), lambda gi, i: (i, gi)),
        scratch_shapes=[pltpu.VMEM((rows, LANES), F32), pltpu.VMEM((rows, 2 * HEAD_DIM), F32),
                        pltpu.VMEM((rows, tk), F32), pltpu.VMEM((rows, tk), F32)],
        compiler_params=_params(("arbitrary", "arbitrary")),
        name="nsa_sel",
    )(q, sel_bias, ks_aug, vs_aug, ocw, gates)


def _slc_from_cmp(n_slc, n_cmp):
    cs = jnp.arange(n_cmp)[None, :] * CMP_STRIDE
    ss = jnp.arange(n_slc)[:, None] * SLC_BLOCK
    ov = jnp.clip(jnp.minimum(cs + CMP_BLOCK, ss + SLC_BLOCK) - jnp.maximum(cs, ss), 0, None)
    return (ov.astype(F32) / CMP_BLOCK).astype(BF16)


def kernel(x, mem, positions, ffn_norm, ffn_w_gate, ffn_w_up, ffn_w_down, mix_norm, mem_norm, w_mem_kv, w_out, w_in_conv, conv_w, w_in_nsa, kv_norm, w_kv, cmp_pos_k, cmp_w1_k, cmp_w2_k, cmp_pos_v, cmp_w1_v, cmp_w2_v, final_norm):
    b, s, d = x.shape
    assert b == 1
    depth = ffn_norm.shape[0]
    n_a = w_in_conv.shape[0]
    mem_dim = w_mem_kv.shape[2] // 2
    conv_ch = conv_w.shape[2]
    kv_w = w_kv.shape[1] // 6
    g = NSA_KV_GROUPS
    n_q = w_in_nsa.shape[2] - mem_dim
    n_heads = n_q // (HEAD_DIM + 3)
    q_dim = n_heads * HEAD_DIM
    rep = n_heads // g
    n_slc = s // SLC_BLOCK
    bf = lambda w: w.astype(BF16)

    h = x[0]
    mem2 = mem[0]
    tables = _rope_tables(positions.reshape(s, 1))

    kv_side = None
    for layer in range(depth):
        if layer == n_a:
            chunks, ks_aug, vs_aug, kw, vw = _kv_side(h, kv_norm, w_kv, tables)
            nc = s // CMP_STRIDE
            pos = bf(jnp.stack([cmp_pos_k, cmp_pos_v]).reshape(2, 1, CMP_BLOCK * HEAD_DIM))
            cmp_kv = _compress(chunks, pos, bf(jnp.stack([cmp_w1_k, cmp_w1_v])), bf(jnp.stack([cmp_w2_k, cmp_w2_v])))
            kv_side = (cmp_kv, ks_aug, vs_aug, kw, vw, _slc_from_cmp(n_slc, nc))

        h = _ffn(h, ffn_norm[layer, 0], ffn_w_gate, ffn_w_up, ffn_w_down, layer, 0)

        if layer < n_a:
            tok, q_mem = _mix_a(h, mix_norm[layer], w_in_conv, conv_w, layer)
        else:
            w_in = w_in_nsa[layer - n_a]
            wg = w_in[:, q_dim:q_dim + 3 * n_heads].reshape(d, g, 3 * rep)
            wg = jnp.pad(wg, ((0, 0), (0, 0), (0, LANES - 3 * rep))).reshape(d, g * LANES)
            w_cat = bf(jnp.concatenate([w_in[:, :q_dim], w_in[:, q_dim + 3 * n_heads:], wg], axis=1))
            q, q_mem, gates = _mix_b(h, mix_norm[layer], w_cat, tables, q_dim=q_dim, mem_dim=mem_dim,
                                     scale=HEAD_DIM ** -0.5 * math.log2(math.e))
            cmp_kv, ks_aug, vs_aug, kw, vw, overlap = kv_side
            ocw, sel_bias = _nsa_cmp(q, gates, cmp_kv, overlap, kw, vw)
            tok = _nsa_sel(q, gates, sel_bias, ks_aug, vs_aug, ocw)

        h = _mix_out(h, tok, q_mem, _mem_kv(mem2, mem_norm[layer], w_mem_kv, layer), w_out, layer)

        last = layer == depth - 1
        h = _ffn(h, ffn_norm[layer, 1], ffn_w_gate, ffn_w_up, ffn_w_down, layer, 1, final_norm if last else None)

    return h[None]
```

```python
import functools
import math

import jax
import jax.numpy as jnp
from jax import lax
from jax.experimental import pallas as pl
from jax.experimental.pallas import tpu as pltpu

HEAD_DIM = 128
ROT_DIM = HEAD_DIM // 4
ROPE_THETA = 500000.0
MEM_HEADS = 4
NSA_KV_GROUPS = 4
CMP_STRIDE = 16
CMP_BLOCK = 2 * CMP_STRIDE
SLC_BLOCK = 64
N_SELECT = 16
N_FORCED = 3
WINDOW = 512
RMS_EPS = 1e-6
CONV_K = 3

LANES = 128
SUBLANES = 8
NEG_BIG = -1e30
VMEM_LIMIT = 56 * 1024 * 1024

F32 = jnp.float32
BF16 = jnp.bfloat16


def _dot(a, b):
    return jnp.dot(a, b, preferred_element_type=F32)


def _dot_nt(a, b):
    return lax.dot_general(a, b, (((1,), (1,)), ((), ())), preferred_element_type=F32)


def _rms(x, g):
    ms = jnp.mean(x * x, axis=-1, keepdims=True)
    return x * lax.rsqrt(ms + RMS_EPS) * g


def _params(sem):
    return pltpu.CompilerParams(dimension_semantics=sem, vmem_limit_bytes=VMEM_LIMIT)


def _rope_table_kernel(pos_ref, inv_ref, cos_ref, sin_ref):
    ang = pos_ref[...].astype(F32) * inv_ref[...]
    lane = lax.broadcasted_iota(jnp.int32, ang.shape, 1)
    c = jnp.cos(ang)
    s = jnp.sin(ang)
    half = ROT_DIM // 2
    cos_ref[...] = jnp.where(lane < ROT_DIM, c, 1.0)
    sin_ref[...] = jnp.where(lane < half, -s, jnp.where(lane < ROT_DIM, s, 0.0))


def _rope_tables(positions_col):
    s = positions_col.shape[0]
    half = ROT_DIM // 2
    inv = 1.0 / (ROPE_THETA ** (jnp.arange(half, dtype=F32) / half))
    inv_row = jnp.concatenate([inv, inv, jnp.zeros((LANES - ROT_DIM,), F32)])[None, :]
    tm = min(s, 1024)
    return pl.pallas_call(
        _rope_table_kernel,
        out_shape=(jax.ShapeDtypeStruct((s, LANES), F32), jax.ShapeDtypeStruct((s, LANES), F32)),
        grid=(s // tm,),
        in_specs=[pl.BlockSpec((tm, 1), lambda i: (i, 0)), pl.BlockSpec((1, LANES), lambda i: (0, 0))],
        out_specs=(pl.BlockSpec((tm, LANES), lambda i: (i, 0)), pl.BlockSpec((tm, LANES), lambda i: (i, 0))),
        compiler_params=_params(("arbitrary",)),
        name="rope_tables",
    )(positions_col, inv_row)


def _rotate(x, cos_t, sin_t):
    half = ROT_DIM // 2
    lane = lax.broadcasted_iota(jnp.int32, x.shape, 1)
    swapped = jnp.where(lane < half, pltpu.roll(x, LANES - half, axis=1), pltpu.roll(x, half, axis=1))
    return x * cos_t + swapped * sin_t


def _store_heads(o_ref, acc, tables):
    for h in range(acc.shape[1] // HEAD_DIM):
        xh = acc[:, h * HEAD_DIM:(h + 1) * HEAD_DIM]
        if tables is not None:
            xh = _rotate(xh, tables[0][...], tables[1][...])
        o_ref[h] = xh.astype(o_ref.dtype)


def _ffn_kernel(x_hbm, g_ref, wg_ref, wu_ref, wd_ref, *rest, final):
    if final:
        fg_ref, o_ref, xn_ref, xbuf_ref, x_sem = rest
    else:
        o_ref, xn_ref, xbuf_ref, x_sem = rest
    i = pl.program_id(0)
    j = pl.program_id(1)
    tm = xbuf_ref.shape[0]

    def x_copy(row_tile):
        return pltpu.make_async_copy(x_hbm.at[pl.ds(row_tile * tm, tm), :], xbuf_ref, x_sem)

    @pl.when((i == 0) & (j == 0))
    def _():
        x_copy(0).start()

    @pl.when(j == 0)
    def _():
        x_copy(i).wait()
        x = xbuf_ref[...]
        xn_ref[...] = _rms(x, g_ref[...]).astype(BF16)
        o_ref[...] = x

    @pl.when((j == 1) & (i + 1 < pl.num_programs(0)))
    def _():
        x_copy(i + 1).start()

    xn = xn_ref[...]
    a = _dot(xn, wg_ref[...].astype(BF16))
    b = _dot(xn, wu_ref[...].astype(BF16))
    mid = (a * jax.nn.sigmoid(a) * b).astype(BF16)
    o_ref[...] += 0.5 * _dot(mid, wd_ref[...].astype(BF16))

    if final:
        @pl.when(j == pl.num_programs(1) - 1)
        def _():
            o_ref[...] = _rms(o_ref[...], fg_ref[...])


def _ffn(x, g, w_gate, w_up, w_down, layer, half, final_g=None, *, tm=1024, tf=256):
    s, d = x.shape
    f = w_gate.shape[3]
    tm = min(tm, s)
    assert f // tf >= 2
    final = final_g is not None
    in_specs = [
        pl.BlockSpec(memory_space=pl.ANY),
        pl.BlockSpec((1, d), lambda i, j: (0, 0)),
        pl.BlockSpec((None, None, d, tf), lambda i, j: (layer, half, 0, j)),
        pl.BlockSpec((None, None, d, tf), lambda i, j: (layer, half, 0, j)),
        pl.BlockSpec((None, None, tf, d), lambda i, j: (layer, half, j, 0)),
    ]
    args = [x, g[None, :], w_gate, w_up, w_down]
    if final:
        in_specs.append(pl.BlockSpec((1, d), lambda i, j: (0, 0)))
        args.append(final_g[None, :])
    return pl.pallas_call(
        functools.partial(_ffn_kernel, final=final),
        out_shape=jax.ShapeDtypeStruct((s, d), F32),
        grid=(s // tm, f // tf),
        in_specs=in_specs,
        out_specs=pl.BlockSpec((tm, d), lambda i, j: (i, 0)),
        scratch_shapes=[pltpu.VMEM((tm, d), BF16), pltpu.VMEM((tm, d), F32), pltpu.SemaphoreType.DMA],
        compiler_params=_params(("arbitrary", "arbitrary")),
        name="ffn_final" if final else "ffn",
    )(*args)


def _mem_kv_kernel(x_ref, g_ref, w_ref, o_ref):
    xn = _rms(x_ref[...], g_ref[...]).astype(BF16)
    o_ref[...] = _dot(xn, w_ref[...].astype(BF16)).astype(o_ref.dtype)


def _mem_kv(mem2, g, w_mem_kv, layer, *, tn=512):
    n_mem, d = mem2.shape
    n = w_mem_kv.shape[2]
    return pl.pallas_call(
        _mem_kv_kernel,
        out_shape=jax.ShapeDtypeStruct((n_mem, n), BF16),
        grid=(n // tn,),
        in_specs=[
            pl.BlockSpec((n_mem, d), lambda j: (0, 0)),
            pl.BlockSpec((1, d), lambda j: (0, 0)),
            pl.BlockSpec((None, d, tn), lambda j: (layer, 0, j)),
        ],
        out_specs=pl.BlockSpec((n_mem, tn), lambda j: (0, j)),
        compiler_params=_params(("arbitrary",)),
        name="mem_kv",
    )(mem2, g[None, :], w_mem_kv)


def _kv_side_kernel(x_ref, g_ref, w_ref, cos_ref, sin_ref, chunk_ref, ks_ref, vs_ref, kw_ref, vw_ref,
                    xn_ref, stage_ref, *, n_grp):
    i = pl.program_id(0)
    j = pl.program_id(1)
    tm = x_ref.shape[0]

    @pl.when(j == 0)
    def _():
        xn_ref[...] = _rms(x_ref[...], g_ref[...]).astype(BF16)

    acc = _dot(xn_ref[...], w_ref[...].astype(BF16))

    def head(h, rot):
        xh = acc[:, h * HEAD_DIM:(h + 1) * HEAD_DIM]
        return _rotate(xh, cos_ref[...], sin_ref[...]) if rot else xh

    def store_chunks(first, rot):
        for h in range(n_grp):
            stage_ref[...] = head(h, rot)
            for p in range(CMP_STRIDE):
                chunk_ref[first + h, :, p * HEAD_DIM:(p + 1) * HEAD_DIM] = (
                    stage_ref[pl.ds(p, tm // CMP_STRIDE, stride=CMP_STRIDE), :].astype(chunk_ref.dtype))

    @pl.when(j == 0)
    def _():
        store_chunks(0, True)

    @pl.when(j == 1)
    def _():
        store_chunks(n_grp, False)

    @pl.when(j == 2)
    def _():
        n_slc = ks_ref.shape[2] - HEAD_DIM
        row_blk = (i * tm + lax.broadcasted_iota(jnp.int32, (tm, n_slc), 0)) // SLC_BLOCK
        code = jnp.where(row_blk == lax.broadcasted_iota(jnp.int32, (tm, n_slc), 1), 1.0, 0.0).astype(ks_ref.dtype)
        for h in range(n_grp):
            ks_ref[h, :, 0:HEAD_DIM] = head(h, True).astype(ks_ref.dtype)
            ks_ref[h, :, HEAD_DIM:] = code

    @pl.when(j == 3)
    def _():
        for h in range(n_grp):
            vs_ref[h, :, 0:HEAD_DIM] = head(h, False).astype(vs_ref.dtype)
            vs_ref[h, :, HEAD_DIM:] = jnp.ones((tm, HEAD_DIM), vs_ref.dtype)

    @pl.when(j == 4)
    def _():
        for h in range(n_grp):
            kw_ref[h] = head(h, True).astype(kw_ref.dtype)

    @pl.when(j == 5)
    def _():
        for h in range(n_grp):
            vw_ref[h] = head(h, False).astype(vw_ref.dtype)


def _kv_side(x, g, w_kv, tables, *, tm=1024):
    s, d = x.shape
    grp = NSA_KV_GROUPS
    tn = grp * HEAD_DIM
    assert w_kv.shape[1] == 6 * tn
    tm = min(tm, s)
    n_slc = s // SLC_BLOCK
    nc = s // CMP_STRIDE
    cw = CMP_STRIDE * HEAD_DIM
    row = lambda i, j: (0, i, 0)
    return pl.pallas_call(
        functools.partial(_kv_side_kernel, n_grp=grp),
        out_shape=(jax.ShapeDtypeStruct((2 * grp, nc, cw), BF16),
                   jax.ShapeDtypeStruct((grp, s, HEAD_DIM + n_slc), BF16),
                   jax.ShapeDtypeStruct((grp, s, 2 * HEAD_DIM), BF16),
                   jax.ShapeDtypeStruct((grp, s, HEAD_DIM), BF16),
                   jax.ShapeDtypeStruct((grp, s, HEAD_DIM), BF16)),
        grid=(s // tm, 6),
        in_specs=[
            pl.BlockSpec((tm, d), lambda i, j: (i, 0)),
            pl.BlockSpec((1, d), lambda i, j: (0, 0)),
            pl.BlockSpec((d, tn), lambda i, j: (0, j)),
            pl.BlockSpec((tm, LANES), lambda i, j: (i, 0)),
            pl.BlockSpec((tm, LANES), lambda i, j: (i, 0)),
        ],
        out_specs=(pl.BlockSpec((2 * grp, tm // CMP_STRIDE, cw), row),
                   pl.BlockSpec((grp, tm, HEAD_DIM + n_slc), row),
                   pl.BlockSpec((grp, tm, 2 * HEAD_DIM), row),
                   pl.BlockSpec((grp, tm, HEAD_DIM), row),
                   pl.BlockSpec((grp, tm, HEAD_DIM), row)),
        scratch_shapes=[pltpu.VMEM((tm, d), BF16), pltpu.VMEM((tm, HEAD_DIM), F32)],
        compiler_params=_params(("arbitrary", "arbitrary")),
        name="kv_side",
    )(x, g[None, :], w_kv, *tables)


def _mix_a_kernel(x_ref, g_ref, wb_ref, wc_ref, wv_ref, wq_ref, cw_ref, tok_ref, qmem_ref,
                  xn_ref, halo_ref, ext_ref):
    i = pl.program_id(0)
    j = pl.program_id(1)
    tm = x_ref.shape[0]
    tc = tok_ref.shape[1]

    @pl.when(j == 0)
    def _():
        xn_ref[...] = _rms(x_ref[...], g_ref[...]).astype(BF16)
        qmem_ref[...] = _dot(xn_ref[...], wq_ref[...].astype(BF16)).astype(qmem_ref.dtype)

    xn = xn_ref[...]
    gate_b = _dot(xn, wb_ref[...].astype(BF16))
    p = _dot(xn, wc_ref[...].astype(BF16)) * _dot(xn, wv_ref[...].astype(BF16))

    @pl.when(i == 0)
    def _():
        halo_ref[j] = jnp.zeros((SUBLANES, tc), F32)

    ext_ref[0:SUBLANES, :] = halo_ref[j]
    ext_ref[SUBLANES:, :] = p
    halo_ref[j] = p[tm - SUBLANES:, :]
    y = cw_ref[CONV_K - 1:CONV_K, :] * p
    for k in range(CONV_K - 1):
        off = SUBLANES - (CONV_K - 1 - k)
        y = y + cw_ref[k:k + 1, :] * ext_ref[off:off + tm, :]
    tok_ref[...] = (gate_b * y).astype(tok_ref.dtype)


def _mix_a(x, g, w_in_conv, conv_w, layer, *, tm=1024, tc=256):
    s, d = x.shape
    n_ch = conv_w.shape[2]
    nt = n_ch // tc
    md = w_in_conv.shape[2] - 3 * n_ch
    assert (3 * n_ch) % md == 0
    tm = min(tm, s)
    w_tile = lambda part: pl.BlockSpec((None, d, tc), lambda i, j: (layer, 0, part * nt + j))
    return pl.pallas_call(
        _mix_a_kernel,
        out_shape=(jax.ShapeDtypeStruct((s, n_ch), BF16), jax.ShapeDtypeStruct((s, md), BF16)),
        grid=(s // tm, nt),
        in_specs=[
            pl.BlockSpec((tm, d), lambda i, j: (i, 0)),
            pl.BlockSpec((1, d), lambda i, j: (0, 0)),
            w_tile(0), w_tile(1), w_tile(2),
            pl.BlockSpec((None, d, md), lambda i, j: (layer, 0, 3 * n_ch // md), pipeline_mode=pl.Buffered(1)),
            pl.BlockSpec((None, CONV_K, tc), lambda i, j: (layer, 0, j)),
        ],
        out_specs=(pl.BlockSpec((tm, tc), lambda i, j: (i, j)), pl.BlockSpec((tm, md), lambda i, j: (i, 0))),
        scratch_shapes=[pltpu.VMEM((tm, d), BF16), pltpu.VMEM((nt, SUBLANES, tc), F32),
                        pltpu.VMEM((tm + SUBLANES, tc), F32)],
        compiler_params=_params(("arbitrary", "arbitrary")),
        name="mix_a_in",
    )(x, g[None, :], w_in_conv, w_in_conv, w_in_conv, w_in_conv, conv_w)


def _nsa_tail_kernel(w_ref, o_ref, *, q_dim, n_gate, mem_dim):
    tail = w_ref[:, q_dim:q_dim + n_gate + mem_dim]
    o_ref[:, 0:mem_dim] = tail[:, n_gate:].astype(o_ref.dtype)
    pad = jnp.zeros((tail.shape[0], LANES - n_gate), F32)
    o_ref[:, mem_dim:] = jnp.concatenate([tail[:, :n_gate], pad], axis=1).astype(o_ref.dtype)


def _nsa_tail(w_in_nsa, layer, *, q_dim, n_gate, mem_dim, tk=256):
    _, d, n = w_in_nsa.shape
    assert n == q_dim + n_gate + mem_dim and q_dim % LANES == 0 and n_gate <= LANES
    return pl.pallas_call(
        functools.partial(_nsa_tail_kernel, q_dim=q_dim, n_gate=n_gate, mem_dim=mem_dim),
        out_shape=jax.ShapeDtypeStruct((d, mem_dim + LANES), BF16),
        grid=(d // tk,),
        in_specs=[pl.BlockSpec((None, tk, n), lambda i: (layer, i, 0))],
        out_specs=pl.BlockSpec((tk, mem_dim + LANES), lambda i: (i, 0)),
        compiler_params=_params(("arbitrary",)),
        name="nsa_tail_w",
    )(w_in_nsa)


def _mix_b_kernel(x_ref, g_ref, wq_ref, wt_ref, cos_ref, sin_ref, q_ref, qmem_ref, gate_ref, xn_ref,
                  *, q_tiles, scale, n_grp, grp_gates):
    j = pl.program_id(1)

    @pl.when(j == 0)
    def _():
        xn_ref[...] = _rms(x_ref[...], g_ref[...]).astype(BF16)

    @pl.when(j < q_tiles)
    def _():
        acc = _dot(xn_ref[...], wq_ref[...].astype(BF16))
        _store_heads(q_ref, acc * scale, (cos_ref, sin_ref))

    @pl.when(j == q_tiles)
    def _():
        md = qmem_ref.shape[1]
        acc = _dot(xn_ref[...], wt_ref[...])
        qmem_ref[...] = acc[:, :md].astype(qmem_ref.dtype)
        gates = jax.nn.sigmoid(acc[:, md:])
        for gi in range(n_grp):
            shift = (LANES - gi * grp_gates) % LANES
            gate_ref[gi] = gates if shift == 0 else pltpu.roll(gates, shift, axis=1)


def _mix_b(x, g, w_in_nsa, w_tail, tables, layer, *, q_dim, mem_dim, scale, n_grp, grp_gates, tm=1024, tn=512):
    s, d = x.shape
    q_tiles = q_dim // tn
    hp = tn // HEAD_DIM
    tm = min(tm, s)
    return pl.pallas_call(
        functools.partial(_mix_b_kernel, q_tiles=q_tiles, scale=scale, n_grp=n_grp, grp_gates=grp_gates),
        out_shape=(jax.ShapeDtypeStruct((q_dim // HEAD_DIM, s, HEAD_DIM), BF16),
                   jax.ShapeDtypeStruct((s, mem_dim), BF16),
                   jax.ShapeDtypeStruct((n_grp, s, LANES), F32)),
        grid=(s // tm, q_tiles + 1),
        in_specs=[
            pl.BlockSpec((tm, d), lambda i, j: (i, 0)),
            pl.BlockSpec((1, d), lambda i, j: (0, 0)),
            pl.BlockSpec((None, d, tn), lambda i, j: (layer, 0, jnp.minimum(j, q_tiles - 1))),
            pl.BlockSpec((d, mem_dim + LANES), lambda i, j: (0, 0), pipeline_mode=pl.Buffered(1)),
            pl.BlockSpec((tm, LANES), lambda i, j: (i, 0)),
            pl.BlockSpec((tm, LANES), lambda i, j: (i, 0)),
        ],
        out_specs=(
            pl.BlockSpec((hp, tm, HEAD_DIM), lambda i, j: (jnp.minimum(j, q_tiles - 1), i, 0)),
            pl.BlockSpec((tm, mem_dim), lambda i, j: (i, 0)),
            pl.BlockSpec((n_grp, tm, LANES), lambda i, j: (0, i, 0)),
        ),
        scratch_shapes=[pltpu.VMEM((tm, d), BF16)],
        compiler_params=_params(("arbitrary", "arbitrary")),
        name="mix_b_in",
    )(x, g[None, :], w_in_nsa, w_tail, *tables)


def _mix_out_kernel(h_ref, tok_ref, qmem_ref, k_ref, v_ref, w1_ref, w2_ref, o_ref, memo_ref):
    @pl.when(pl.program_id(1) == 0)
    def _():
        scale = HEAD_DIM ** -0.5
        for h in range(MEM_HEADS):
            sl = slice(h * HEAD_DIM, (h + 1) * HEAD_DIM)
            s = _dot_nt(qmem_ref[:, sl], k_ref[:, sl]) * scale
            e = jnp.exp(s - jnp.max(s, axis=-1, keepdims=True))
            o = _dot(e.astype(BF16), v_ref[:, sl]) * (1.0 / jnp.sum(e, axis=-1, keepdims=True))
            memo_ref[:, sl] = o.astype(memo_ref.dtype)

    o_ref[...] = (h_ref[...] + _dot(tok_ref[...], w1_ref[...].astype(BF16))
                  + _dot(memo_ref[...], w2_ref[...].astype(BF16)))


def _mix_out(h, tok, q_mem, mem_kv, w_out, layer, *, tm=2048, tn=512):
    s, d = h.shape
    n_tok = tok.shape[1]
    md = q_mem.shape[1]
    n_mem = mem_kv.shape[0]
    assert n_tok % md == 0
    tm = min(tm, s)
    return pl.pallas_call(
        _mix_out_kernel,
        out_shape=jax.ShapeDtypeStruct((s, d), F32),
        grid=(s // tm, d // tn),
        in_specs=[
            pl.BlockSpec((tm, tn), lambda i, j: (i, j)),
            pl.BlockSpec((tm, n_tok), lambda i, j: (i, 0)),
            pl.BlockSpec((tm, md), lambda i, j: (i, 0)),
            pl.BlockSpec((n_mem, md), lambda i, j: (0, 0)),
            pl.BlockSpec((n_mem, md), lambda i, j: (0, 1)),
            pl.BlockSpec((None, n_tok, tn), lambda i, j: (layer, 0, j)),
            pl.BlockSpec((None, md, tn), lambda i, j: (layer, n_tok // md, j)),
        ],
        out_specs=pl.BlockSpec((tm, tn), lambda i, j: (i, j)),
        scratch_shapes=[pltpu.VMEM((tm, md), BF16)],
        compiler_params=_params(("arbitrary", "arbitrary")),
        name="mix_out",
    )(h, tok, q_mem, mem_kv, mem_kv, w_out, w_out)


def _compress_kernel(c_ref, pos_ref, w1_ref, w2_ref, o_ref):
    x = c_ref[0]
    nc, half = x.shape
    w1a = w1_ref[0, 0:half, :]
    w1b = w1_ref[0, half:, :]
    pos = jnp.broadcast_to(pos_ref[0], (SUBLANES, 2 * half))
    bias = _dot(pos, w1_ref[0])[0:1, :]
    first = _dot(x, w1a)
    second = pltpu.roll(_dot(x, w1b), nc - 1, axis=0)
    hid = jax.nn.gelu(first + second + bias, approximate=True)
    out = _dot(hid.astype(BF16), w2_ref[0])
    row = lax.broadcasted_iota(jnp.int32, out.shape, 0)
    o_ref[0] = jnp.where(row < nc - 1, out, 0.0).astype(o_ref.dtype)


def _compress(chunks, pos, w1, w2):
    n, nc, cw = chunks.shape
    g = n // 2
    return pl.pallas_call(
        _compress_kernel,
        out_shape=jax.ShapeDtypeStruct((2 * g, nc, HEAD_DIM), BF16),
        grid=(2 * g,),
        in_specs=[
            pl.BlockSpec((1, nc, cw), lambda i: (i, 0, 0)),
            pl.BlockSpec((1, 1, 2 * cw), lambda i: (i // g, 0, 0)),
            pl.BlockSpec((1, 2 * cw, HEAD_DIM), lambda i: (i // g, 0, 0)),
            pl.BlockSpec((1, HEAD_DIM, HEAD_DIM), lambda i: (i // g, 0, 0)),
        ],
        out_specs=pl.BlockSpec((1, nc, HEAD_DIM), lambda i: (i, 0, 0)),
        compiler_params=_params(("arbitrary",)),
        name="compress",
    )(chunks, pos, w1, w2)


def _softmax2_parts(s, bias):
    s = s + bias
    e = jnp.exp2(s - jnp.max(s, axis=-1, keepdims=True))
    return e, 1.0 / jnp.sum(e, axis=-1, keepdims=True)


def _nsa_cmp_kernel(q_ref, gate_ref, kc_ref, vc_ref, ovt_ref, kw_ref, vw_ref, ocw_ref, bias_ref,
                    *, tq, tw, rep, n_pick):
    i = pl.program_id(1)
    t0 = i * tq
    rows = rep * tq
    q3 = q_ref[...].reshape(rows, HEAD_DIM)
    gates = gate_ref[...]

    kc = kc_ref[0]
    n_cmp = kc.shape[0]
    cmp_end = lax.broadcasted_iota(jnp.int32, (tq, n_cmp), 1) * CMP_STRIDE + (CMP_BLOCK - 1)
    t_c = t0 + lax.broadcasted_iota(jnp.int32, (tq, n_cmp), 0)
    bias_c = jnp.where(cmp_end <= t_c, 0.0, NEG_BIG)
    sees_any = t0 + lax.broadcasted_iota(jnp.int32, (tq, 1), 0) >= CMP_BLOCK - 1
    o_c, inv_c, p_sum = [], [], None
    for r in range(rep):
        e_r, inv_r = _softmax2_parts(_dot_nt(q_ref[r], kc), bias_c)
        inv_r = jnp.where(sees_any, inv_r, 0.0)
        o_c.append(_dot(e_r.astype(BF16), vc_ref[0]))
        inv_c.append(inv_r)
        p_r = e_r * inv_r
        p_sum = p_r if p_sum is None else p_sum + p_r

    p_hi = p_sum.astype(BF16)
    p_lo = (p_sum - p_hi.astype(F32)).astype(BF16)
    imp_all = _dot_nt(ovt_ref[...], p_hi) + _dot_nt(ovt_ref[...], p_lo)
    n_slc = imp_all.shape[0]
    blk = lax.broadcasted_iota(jnp.int32, (n_slc, LANES), 0)
    blk_f = blk.astype(F32)
    for c in range(tq // LANES):
        imp = imp_all[:, c * LANES:(c + 1) * LANES]
        jt = (t0 + c * LANES + lax.broadcasted_iota(jnp.int32, imp.shape, 1)) // SLC_BLOCK
        valid = blk <= jt
        forced = (blk == 0) | (blk == jt) | (blk == jt - 1)
        work = jnp.where(valid, jnp.where(forced, -jnp.inf, imp), -jnp.inf)
        sel_bias = jnp.where(forced, 0.0, NEG_BIG)
        for _ in range(n_pick):
            mx = jnp.max(work, axis=0, keepdims=True)
            first = jnp.min(jnp.where(work == mx, blk_f, float(n_slc)), axis=0, keepdims=True)
            pick = blk_f == first
            sel_bias = jnp.where(pick, 0.0, sel_bias)
            work = jnp.where(pick, -jnp.inf, work)
        bias_ref[0, c * LANES:(c + 1) * LANES, :] = jnp.where(valid, sel_bias, NEG_BIG).T.astype(bias_ref.dtype)

    wlen = tw + WINDOW
    rel = lax.broadcasted_iota(jnp.int32, (tw, wlen), 1) - lax.broadcasted_iota(jnp.int32, (tw, wlen), 0)
    for sub in range(tq // tw):
        ts0 = t0 + sub * tw
        ws = pl.multiple_of(jnp.maximum(ts0 - WINDOW, 0), tw)
        qs = jnp.concatenate([q_ref[r, sub * tw:(sub + 1) * tw, :] for r in range(rep)], axis=0)
        s_w = _dot_nt(qs, kw_ref[0, pl.ds(ws, wlen), :])
        rel_w = rel + (ws - ts0)
        bias_w = jnp.where((rel_w <= 0) & (rel_w > -WINDOW), 0.0, NEG_BIG)
        e_w, inv_w = _softmax2_parts(s_w, jnp.concatenate([bias_w] * rep, axis=0))
        o_w = _dot(e_w.astype(BF16), vw_ref[0, pl.ds(ws, wlen), :])
        gs = gates[sub * tw:(sub + 1) * tw]
        for r in range(rep):
            cr = slice(sub * tw, (sub + 1) * tw)
            wr = slice(r * tw, (r + 1) * tw)
            ocw_ref[cr, r * HEAD_DIM:(r + 1) * HEAD_DIM] = (
                (gs[:, 3 * r:3 * r + 1] * inv_c[r][cr]) * o_c[r][cr]
                + (gs[:, 3 * r + 2:3 * r + 3] * inv_w[wr]) * o_w[wr])


def _nsa_cmp(q, gates, cmp_kv, overlap, kw, vw, *, tq=512, tw=128):
    n_h, s, _ = q.shape
    g = NSA_KV_GROUPS
    rep = n_h // g
    nc = cmp_kv.shape[1]
    n_slc = s // SLC_BLOCK
    tq = min(tq, s)
    n_pick = max(min(N_SELECT, n_slc) - N_FORCED, 0)
    kernel = functools.partial(_nsa_cmp_kernel, tq=tq, tw=tw, rep=rep, n_pick=n_pick)
    return pl.pallas_call(
        kernel,
        out_shape=(jax.ShapeDtypeStruct((s, n_h * HEAD_DIM), F32), jax.ShapeDtypeStruct((g, s, n_slc), BF16)),
        grid=(g, s // tq),
        in_specs=[
            pl.BlockSpec((rep, tq, HEAD_DIM), lambda gi, i: (gi, i, 0)),
            pl.BlockSpec((None, tq, LANES), lambda gi, i: (gi, i, 0)),
            pl.BlockSpec((1, nc, HEAD_DIM), lambda gi, i: (gi, 0, 0)),
            pl.BlockSpec((1, nc, HEAD_DIM), lambda gi, i: (g + gi, 0, 0)),
            pl.BlockSpec((n_slc, nc), lambda gi, i: (0, 0)),
            pl.BlockSpec((1, s, HEAD_DIM), lambda gi, i: (gi, 0, 0)),
            pl.BlockSpec((1, s, HEAD_DIM), lambda gi, i: (gi, 0, 0)),
        ],
        out_specs=(pl.BlockSpec((tq, rep * HEAD_DIM), lambda gi, i: (i, gi)),
                   pl.BlockSpec((1, tq, n_slc), lambda gi, i: (gi, i, 0))),
        compiler_params=_params(("arbitrary", "arbitrary")),
        name="nsa_cmp_win",
    )(q, gates, cmp_kv, cmp_kv, overlap, kw, vw)


def _nsa_sel_kernel(q_ref, bias_ref, ks_ref, vs_ref, ocw_ref, gate_ref, o_ref, m_ref, acc_ref, sa_ref, sb_ref,
                    *, tq, tk, rep):
    i = pl.program_id(1)
    t0 = i * tq
    rows = rep * tq
    q3 = q_ref[...].reshape(rows, HEAD_DIM)
    q_aug = jnp.concatenate([q3, jnp.concatenate([bias_ref[0]] * rep, axis=0)], axis=1)

    m_ref[...] = jnp.full(m_ref.shape, NEG_BIG, F32)
    acc_ref[...] = jnp.zeros(acc_ref.shape, F32)

    def scores(kt, s_ref):
        k_aug = ks_ref[0, pl.ds(pl.multiple_of(kt * tk, tk), tk), :]
        for r in range(rep):
            rs = slice(r * tq, (r + 1) * tq)
            s_ref[rs, :] = _dot_nt(q_aug[rs], k_aug)

    def consume(kt, s_ref, causal):
        k0 = pl.multiple_of(kt * tk, tk)
        v_aug = vs_ref[0, pl.ds(k0, tk), :]
        for r in range(rep):
            rs = slice(r * tq, (r + 1) * tq)
            s = s_ref[rs, :]
            if causal:
                t_s = t0 + lax.broadcasted_iota(jnp.int32, s.shape, 0)
                s = jnp.where(lax.broadcasted_iota(jnp.int32, s.shape, 1) + k0 <= t_s, s, NEG_BIG)
            m_prev = m_ref[rs, :]
            m_new = jnp.maximum(m_prev, jnp.max(s, axis=-1, keepdims=True))
            alpha = jnp.exp2(m_prev - m_new)
            p = jnp.exp2(s - jnp.concatenate([m_new] * (tk // LANES), axis=1))
            acc_ref[rs, :] = (jnp.concatenate([alpha, alpha], axis=1) * acc_ref[rs, :]
                              + _dot(p.astype(BF16), v_aug))
            m_ref[rs, :] = m_new

    last = (t0 + tq + tk - 1) // tk - 1
    n_pairs = last // 2

    scores(0, sa_ref)

    def pair(a, carry):
        kt = 2 * a
        scores(kt + 1, sb_ref)
        consume(kt, sa_ref, False)
        scores(kt + 2, sa_ref)
        consume(kt + 1, sb_ref, False)
        return carry

    lax.fori_loop(0, n_pairs, pair, 0)

    @pl.when(last == 2 * n_pairs)
    def _():
        consume(last, sa_ref, True)

    @pl.when(last != 2 * n_pairs)
    def _():
        scores(last, sb_ref)
        consume(last - 1, sa_ref, False)
        consume(last, sb_ref, True)

    acc = acc_ref[...]
    o_s = acc[:, :HEAD_DIM] / acc[:, HEAD_DIM:]
    gates = gate_ref[...]
    for r in range(rep):
        cs = slice(r * HEAD_DIM, (r + 1) * HEAD_DIM)
        o_ref[:, cs] = (ocw_ref[:, cs] + gates[:, 3 * r + 1:3 * r + 2] * o_s[r * tq:(r + 1) * tq]).astype(o_ref.dtype)


def _nsa_sel(q, gates, sel_bias, ks_aug, vs_aug, ocw, *, tq=512, tk=512):
    n_h, s, _ = q.shape
    g = NSA_KV_GROUPS
    rep = n_h // g
    n_slc = sel_bias.shape[2]
    tk = min(tk, s)
    tq = min(tq, tk)
    rows = rep * tq
    kernel = functools.partial(_nsa_sel_kernel, tq=tq, tk=tk, rep=rep)
    return pl.pallas_call(
        kernel,
        out_shape=jax.ShapeDtypeStruct((s, n_h * HEAD_DIM), BF16),
        grid=(g, s // tq),
        in_specs=[
            pl.BlockSpec((rep, tq, HEAD_DIM), lambda gi, i: (gi, i, 0)),
            pl.BlockSpec((1, tq, n_slc), lambda gi, i: (gi, i, 0)),
            pl.BlockSpec((1, s, HEAD_DIM + n_slc), lambda gi, i: (gi, 0, 0)),
            pl.BlockSpec((1, s, 2 * HEAD_DIM), lambda gi, i: (gi, 0, 0)),
            pl.BlockSpec((tq, rep * HEAD_DIM), lambda gi, i: (i, gi)),
            pl.BlockSpec((None, tq, LANES), lambda gi, i: (gi, i, 0)),
        ],
        out_specs=pl.BlockSpec((tq, rep * HEAD_DIM), lambda gi, i: (i, gi)),
        scratch_shapes=[pltpu.VMEM((rows, LANES), F32), pltpu.VMEM((rows, 2 * HEAD_DIM), F32),
                        pltpu.VMEM((rows, tk), F32), pltpu.VMEM((rows, tk), F32)],
        compiler_params=_params(("arbitrary", "arbitrary")),
        name="nsa_sel",
    )(q, sel_bias, ks_aug, vs_aug, ocw, gates)


def _slc_from_cmp(n_slc, n_cmp):
    cs = jnp.arange(n_cmp)[None, :] * CMP_STRIDE
    ss = jnp.arange(n_slc)[:, None] * SLC_BLOCK
    ov = jnp.clip(jnp.minimum(cs + CMP_BLOCK, ss + SLC_BLOCK) - jnp.maximum(cs, ss), 0, None)
    return (ov.astype(F32) / CMP_BLOCK).astype(BF16)


def kernel(x, mem, positions, ffn_norm, ffn_w_gate, ffn_w_up, ffn_w_down, mix_norm, mem_norm, w_mem_kv, w_out, w_in_conv, conv_w, w_in_nsa, kv_norm, w_kv, cmp_pos_k, cmp_w1_k, cmp_w2_k, cmp_pos_v, cmp_w1_v, cmp_w2_v, final_norm):
    b, s, d = x.shape
    assert b == 1
    depth = ffn_norm.shape[0]
    n_a = w_in_conv.shape[0]
    mem_dim = w_mem_kv.shape[2] // 2
    conv_ch = conv_w.shape[2]
    kv_w = w_kv.shape[1] // 6
    g = NSA_KV_GROUPS
    n_q = w_in_nsa.shape[2] - mem_dim
    n_heads = n_q // (HEAD_DIM + 3)
    q_dim = n_heads * HEAD_DIM
    rep = n_heads // g
    n_slc = s // SLC_BLOCK
    bf = lambda w: w.astype(BF16)

    h = x[0]
    mem2 = mem[0]
    tables = _rope_tables(positions.reshape(s, 1))

    kv_side = None
    for layer in range(depth):
        if layer == n_a:
            chunks, ks_aug, vs_aug, kw, vw = _kv_side(h, kv_norm, w_kv, tables)
            nc = s // CMP_STRIDE
            pos = bf(jnp.stack([cmp_pos_k, cmp_pos_v]).reshape(2, 1, CMP_BLOCK * HEAD_DIM))
            cmp_kv = _compress(chunks, pos, bf(jnp.stack([cmp_w1_k, cmp_w1_v])), bf(jnp.stack([cmp_w2_k, cmp_w2_v])))
            kv_side = (cmp_kv, ks_aug, vs_aug, kw, vw, _slc_from_cmp(n_slc, nc))

        h = _ffn(h, ffn_norm[layer, 0], ffn_w_gate, ffn_w_up, ffn_w_down, layer, 0)

        if layer < n_a:
            tok, q_mem = _mix_a(h, mix_norm[layer], w_in_conv, conv_w, layer)
        else:
            w_tail = _nsa_tail(w_in_nsa, layer - n_a, q_dim=q_dim, n_gate=3 * n_heads, mem_dim=mem_dim)
            q, q_mem, gates = _mix_b(h, mix_norm[layer], w_in_nsa, w_tail, tables, layer - n_a, q_dim=q_dim,
                                     mem_dim=mem_dim, scale=HEAD_DIM ** -0.5 * math.log2(math.e),
                                     n_grp=g, grp_gates=3 * rep)
            cmp_kv, ks_aug, vs_aug, kw, vw, overlap = kv_side
            ocw, sel_bias = _nsa_cmp(q, gates, cmp_kv, overlap, kw, vw)
            tok = _nsa_sel(q, gates, sel_bias, ks_aug, vs_aug, ocw)

        h = _mix_out(h, tok, q_mem, _mem_kv(mem2, mem_norm[layer], w_mem_kv, layer), w_out, layer)

        last = layer == depth - 1
        h = _ffn(h, ffn_norm[layer, 1], ffn_w_gate, ffn_w_up, ffn_w_down, layer, 1, final_norm if last else None)

    return h[None]
```

```python
import functools
import math

import jax
import jax.numpy as jnp
from jax import lax
from jax.experimental import pallas as pl
from jax.experimental.pallas import tpu as pltpu

HEAD_DIM = 128
ROT_DIM = HEAD_DIM // 4
ROPE_THETA = 500000.0
MEM_HEADS = 4
NSA_KV_GROUPS = 4
CMP_STRIDE = 16
CMP_BLOCK = 2 * CMP_STRIDE
SLC_BLOCK = 64
N_SELECT = 16
N_FORCED = 3
WINDOW = 512
RMS_EPS = 1e-6
CONV_K = 3

LANES = 128
SUBLANES = 8
NEG_BIG = -1e30
VMEM_LIMIT = 56 * 1024 * 1024

F32 = jnp.float32
BF16 = jnp.bfloat16


def _dot(a, b):
    return jnp.dot(a, b, preferred_element_type=F32)


def _dot_nt(a, b):
    return lax.dot_general(a, b, (((1,), (1,)), ((), ())), preferred_element_type=F32)


def _rms(x, g):
    ms = jnp.mean(x * x, axis=-1, keepdims=True)
    return x * lax.rsqrt(ms + RMS_EPS) * g


def _params(sem):
    return pltpu.CompilerParams(dimension_semantics=sem, vmem_limit_bytes=VMEM_LIMIT)


def _rope_table_kernel(pos_ref, inv_ref, cos_ref, sin_ref):
    ang = pos_ref[...].astype(F32) * inv_ref[...]
    lane = lax.broadcasted_iota(jnp.int32, ang.shape, 1)
    c = jnp.cos(ang)
    s = jnp.sin(ang)
    half = ROT_DIM // 2
    cos_ref[...] = jnp.where(lane < ROT_DIM, c, 1.0)
    sin_ref[...] = jnp.where(lane < half, -s, jnp.where(lane < ROT_DIM, s, 0.0))


def _rope_tables(positions_col):
    s = positions_col.shape[0]
    half = ROT_DIM // 2
    inv = 1.0 / (ROPE_THETA ** (jnp.arange(half, dtype=F32) / half))
    inv_row = jnp.concatenate([inv, inv, jnp.zeros((LANES - ROT_DIM,), F32)])[None, :]
    tm = min(s, 1024)
    return pl.pallas_call(
        _rope_table_kernel,
        out_shape=(jax.ShapeDtypeStruct((s, LANES), F32), jax.ShapeDtypeStruct((s, LANES), F32)),
        grid=(s // tm,),
        in_specs=[pl.BlockSpec((tm, 1), lambda i: (i, 0)), pl.BlockSpec((1, LANES), lambda i: (0, 0))],
        out_specs=(pl.BlockSpec((tm, LANES), lambda i: (i, 0)), pl.BlockSpec((tm, LANES), lambda i: (i, 0))),
        compiler_params=_params(("arbitrary",)),
        name="rope_tables",
    )(positions_col, inv_row)


def _rotate(x, cos_t, sin_t):
    half = ROT_DIM // 2
    lane = lax.broadcasted_iota(jnp.int32, x.shape, 1)
    swapped = jnp.where(lane < half, pltpu.roll(x, LANES - half, axis=1), pltpu.roll(x, half, axis=1))
    return x * cos_t + swapped * sin_t


def _store_heads(o_ref, acc, tables):
    for h in range(acc.shape[1] // HEAD_DIM):
        xh = acc[:, h * HEAD_DIM:(h + 1) * HEAD_DIM]
        if tables is not None:
            xh = _rotate(xh, tables[0][...], tables[1][...])
        o_ref[h] = xh.astype(o_ref.dtype)


def _ffn_kernel(x_hbm, g_ref, wg_ref, wu_ref, wd_ref, *rest, final):
    if final:
        fg_ref, o_ref, xn_ref, xbuf_ref, x_sem = rest
    else:
        o_ref, xn_ref, xbuf_ref, x_sem = rest
    i = pl.program_id(0)
    j = pl.program_id(1)
    tm = xbuf_ref.shape[0]

    def x_copy(row_tile):
        return pltpu.make_async_copy(x_hbm.at[pl.ds(row_tile * tm, tm), :], xbuf_ref, x_sem)

    @pl.when((i == 0) & (j == 0))
    def _():
        x_copy(0).start()

    @pl.when(j == 0)
    def _():
        x_copy(i).wait()
        x = xbuf_ref[...]
        xn_ref[...] = _rms(x, g_ref[...]).astype(BF16)
        o_ref[...] = x

    @pl.when((j == 1) & (i + 1 < pl.num_programs(0)))
    def _():
        x_copy(i + 1).start()

    xn = xn_ref[...]
    a = _dot(xn, wg_ref[...].astype(BF16))
    b = _dot(xn, wu_ref[...].astype(BF16))
    mid = (a * jax.nn.sigmoid(a) * b).astype(BF16)
    o_ref[...] += 0.5 * _dot(mid, wd_ref[...].astype(BF16))

    if final:
        @pl.when(j == pl.num_programs(1) - 1)
        def _():
            o_ref[...] = _rms(o_ref[...], fg_ref[...])


def _ffn(x, g, w_gate, w_up, w_down, layer, half, final_g=None, *, tm=1024, tf=256):
    s, d = x.shape
    f = w_gate.shape[3]
    tm = min(tm, s)
    assert f // tf >= 2
    final = final_g is not None
    in_specs = [
        pl.BlockSpec(memory_space=pl.ANY),
        pl.BlockSpec((1, d), lambda i, j: (0, 0)),
        pl.BlockSpec((None, None, d, tf), lambda i, j: (layer, half, 0, j)),
        pl.BlockSpec((None, None, d, tf), lambda i, j: (layer, half, 0, j)),
        pl.BlockSpec((None, None, tf, d), lambda i, j: (layer, half, j, 0)),
    ]
    args = [x, g[None, :], w_gate, w_up, w_down]
    if final:
        in_specs.append(pl.BlockSpec((1, d), lambda i, j: (0, 0)))
        args.append(final_g[None, :])
    return pl.pallas_call(
        functools.partial(_ffn_kernel, final=final),
        out_shape=jax.ShapeDtypeStruct((s, d), F32),
        grid=(s // tm, f // tf),
        in_specs=in_specs,
        out_specs=pl.BlockSpec((tm, d), lambda i, j: (i, 0)),
        scratch_shapes=[pltpu.VMEM((tm, d), BF16), pltpu.VMEM((tm, d), F32), pltpu.SemaphoreType.DMA],
        compiler_params=_params(("arbitrary", "arbitrary")),
        name="ffn_final" if final else "ffn",
    )(*args)


def _mem_kv_kernel(x_ref, g_ref, w_ref, o_ref):
    xn = _rms(x_ref[...], g_ref[...]).astype(BF16)
    o_ref[...] = _dot(xn, w_ref[...].astype(BF16)).astype(o_ref.dtype)


def _mem_kv(mem2, g, w_mem_kv, layer, *, tn=512):
    n_mem, d = mem2.shape
    n = w_mem_kv.shape[2]
    return pl.pallas_call(
        _mem_kv_kernel,
        out_shape=jax.ShapeDtypeStruct((n_mem, n), BF16),
        grid=(n // tn,),
        in_specs=[
            pl.BlockSpec((n_mem, d), lambda j: (0, 0)),
            pl.BlockSpec((1, d), lambda j: (0, 0)),
            pl.BlockSpec((None, d, tn), lambda j: (layer, 0, j)),
        ],
        out_specs=pl.BlockSpec((n_mem, tn), lambda j: (0, j)),
        compiler_params=_params(("arbitrary",)),
        name="mem_kv",
    )(mem2, g[None, :], w_mem_kv)


def _kv_side_kernel(x_ref, g_ref, w_ref, cos_ref, sin_ref, chunk_ref, ks_ref, vs_ref, kw_ref, vw_ref,
                    xn_ref, stage_ref, *, n_grp):
    i = pl.program_id(0)
    j = pl.program_id(1)
    tm = x_ref.shape[0]

    @pl.when(j == 0)
    def _():
        xn_ref[...] = _rms(x_ref[...], g_ref[...]).astype(BF16)

    acc = _dot(xn_ref[...], w_ref[...].astype(BF16))

    def head(h, rot):
        xh = acc[:, h * HEAD_DIM:(h + 1) * HEAD_DIM]
        return _rotate(xh, cos_ref[...], sin_ref[...]) if rot else xh

    def store_chunks(first, rot):
        for h in range(n_grp):
            stage_ref[...] = head(h, rot)
            for p in range(CMP_STRIDE):
                chunk_ref[first + h, :, p * HEAD_DIM:(p + 1) * HEAD_DIM] = (
                    stage_ref[pl.ds(p, tm // CMP_STRIDE, stride=CMP_STRIDE), :].astype(chunk_ref.dtype))

    @pl.when(j == 0)
    def _():
        store_chunks(0, True)

    @pl.when(j == 1)
    def _():
        store_chunks(n_grp, False)

    @pl.when(j == 2)
    def _():
        n_slc = ks_ref.shape[2] - HEAD_DIM
        row_blk = (i * tm + lax.broadcasted_iota(jnp.int32, (tm, n_slc), 0)) // SLC_BLOCK
        code = jnp.where(row_blk == lax.broadcasted_iota(jnp.int32, (tm, n_slc), 1), 1.0, 0.0).astype(ks_ref.dtype)
        for h in range(n_grp):
            ks_ref[h, :, 0:HEAD_DIM] = head(h, True).astype(ks_ref.dtype)
            ks_ref[h, :, HEAD_DIM:] = code

    @pl.when(j == 3)
    def _():
        for h in range(n_grp):
            vs_ref[h, :, 0:HEAD_DIM] = head(h, False).astype(vs_ref.dtype)
            vs_ref[h, :, HEAD_DIM:] = jnp.ones((tm, HEAD_DIM), vs_ref.dtype)

    @pl.when(j == 4)
    def _():
        for h in range(n_grp):
            kw_ref[h] = head(h, True).astype(kw_ref.dtype)

    @pl.when(j == 5)
    def _():
        for h in range(n_grp):
            vw_ref[h] = head(h, False).astype(vw_ref.dtype)


def _kv_side(x, g, w_kv, tables, *, tm=1024):
    s, d = x.shape
    grp = NSA_KV_GROUPS
    tn = grp * HEAD_DIM
    assert w_kv.shape[1] == 6 * tn
    tm = min(tm, s)
    n_slc = s // SLC_BLOCK
    nc = s // CMP_STRIDE
    cw = CMP_STRIDE * HEAD_DIM
    row = lambda i, j: (0, i, 0)
    return pl.pallas_call(
        functools.partial(_kv_side_kernel, n_grp=grp),
        out_shape=(jax.ShapeDtypeStruct((2 * grp, nc, cw), BF16),
                   jax.ShapeDtypeStruct((grp, s, HEAD_DIM + n_slc), BF16),
                   jax.ShapeDtypeStruct((grp, s, 2 * HEAD_DIM), BF16),
                   jax.ShapeDtypeStruct((grp, s, HEAD_DIM), BF16),
                   jax.ShapeDtypeStruct((grp, s, HEAD_DIM), BF16)),
        grid=(s // tm, 6),
        in_specs=[
            pl.BlockSpec((tm, d), lambda i, j: (i, 0)),
            pl.BlockSpec((1, d), lambda i, j: (0, 0)),
            pl.BlockSpec((d, tn), lambda i, j: (0, j)),
            pl.BlockSpec((tm, LANES), lambda i, j: (i, 0)),
            pl.BlockSpec((tm, LANES), lambda i, j: (i, 0)),
        ],
        out_specs=(pl.BlockSpec((2 * grp, tm // CMP_STRIDE, cw), row),
                   pl.BlockSpec((grp, tm, HEAD_DIM + n_slc), row),
                   pl.BlockSpec((grp, tm, 2 * HEAD_DIM), row),
                   pl.BlockSpec((grp, tm, HEAD_DIM), row),
                   pl.BlockSpec((grp, tm, HEAD_DIM), row)),
        scratch_shapes=[pltpu.VMEM((tm, d), BF16), pltpu.VMEM((tm, HEAD_DIM), F32)],
        compiler_params=_params(("arbitrary", "arbitrary")),
        name="kv_side",
    )(x, g[None, :], w_kv, *tables)


def _mix_a_kernel(x_ref, g_ref, wb_ref, wc_ref, wv_ref, wq_ref, cw_ref, tok_ref, qmem_ref,
                  xn_ref, halo_ref, ext_ref):
    i = pl.program_id(0)
    j = pl.program_id(1)
    tm = x_ref.shape[0]
    tc = tok_ref.shape[1]

    @pl.when(j == 0)
    def _():
        xn_ref[...] = _rms(x_ref[...], g_ref[...]).astype(BF16)
        qmem_ref[...] = _dot(xn_ref[...], wq_ref[...].astype(BF16)).astype(qmem_ref.dtype)

    xn = xn_ref[...]
    gate_b = _dot(xn, wb_ref[...].astype(BF16))
    p = _dot(xn, wc_ref[...].astype(BF16)) * _dot(xn, wv_ref[...].astype(BF16))

    @pl.when(i == 0)
    def _():
        halo_ref[j] = jnp.zeros((SUBLANES, tc), F32)

    ext_ref[0:SUBLANES, :] = halo_ref[j]
    ext_ref[SUBLANES:, :] = p
    halo_ref[j] = p[tm - SUBLANES:, :]
    y = cw_ref[CONV_K - 1:CONV_K, :] * p
    for k in range(CONV_K - 1):
        off = SUBLANES - (CONV_K - 1 - k)
        y = y + cw_ref[k:k + 1, :] * ext_ref[off:off + tm, :]
    tok_ref[...] = (gate_b * y).astype(tok_ref.dtype)


def _mix_a(x, g, w_in_conv, conv_w, layer, *, tm=1024, tc=256):
    s, d = x.shape
    n_ch = conv_w.shape[2]
    nt = n_ch // tc
    md = w_in_conv.shape[2] - 3 * n_ch
    assert (3 * n_ch) % md == 0
    tm = min(tm, s)
    w_tile = lambda part: pl.BlockSpec((None, d, tc), lambda i, j: (layer, 0, part * nt + j))
    return pl.pallas_call(
        _mix_a_kernel,
        out_shape=(jax.ShapeDtypeStruct((s, n_ch), BF16), jax.ShapeDtypeStruct((s, md), BF16)),
        grid=(s // tm, nt),
        in_specs=[
            pl.BlockSpec((tm, d), lambda i, j: (i, 0)),
            pl.BlockSpec((1, d), lambda i, j: (0, 0)),
            w_tile(0), w_tile(1), w_tile(2),
            pl.BlockSpec((None, d, md), lambda i, j: (layer, 0, 3 * n_ch // md), pipeline_mode=pl.Buffered(1)),
            pl.BlockSpec((None, CONV_K, tc), lambda i, j: (layer, 0, j)),
        ],
        out_specs=(pl.BlockSpec((tm, tc), lambda i, j: (i, j)), pl.BlockSpec((tm, md), lambda i, j: (i, 0))),
        scratch_shapes=[pltpu.VMEM((tm, d), BF16), pltpu.VMEM((nt, SUBLANES, tc), F32),
                        pltpu.VMEM((tm + SUBLANES, tc), F32)],
        compiler_params=_params(("arbitrary", "arbitrary")),
        name="mix_a_in",
    )(x, g[None, :], w_in_conv, w_in_conv, w_in_conv, w_in_conv, conv_w)


def _nsa_tail_kernel(w_ref, o_ref, *, q_dim, n_gate, mem_dim):
    o_ref[0:mem_dim, :] = w_ref[q_dim + n_gate:q_dim + n_gate + mem_dim, :]
    o_ref[mem_dim:mem_dim + n_gate, :] = w_ref[q_dim:q_dim + n_gate, :]
    o_ref[mem_dim + n_gate:, :] = jnp.zeros((LANES - n_gate, o_ref.shape[1]), o_ref.dtype)


def _nsa_tail(w_t, layer, *, q_dim, n_gate, mem_dim, tk=256):
    _, n, d = w_t.shape
    assert n == q_dim + n_gate + mem_dim and n_gate <= LANES
    return pl.pallas_call(
        functools.partial(_nsa_tail_kernel, q_dim=q_dim, n_gate=n_gate, mem_dim=mem_dim),
        out_shape=jax.ShapeDtypeStruct((mem_dim + LANES, d), F32),
        grid=(d // tk,),
        in_specs=[pl.BlockSpec((None, n, tk), lambda i: (layer, 0, i))],
        out_specs=pl.BlockSpec((mem_dim + LANES, tk), lambda i: (0, i)),
        compiler_params=_params(("arbitrary",)),
        name="nsa_tail_w",
    )(w_t)


def _mix_b_kernel(x_ref, g_ref, wq_ref, wt_ref, cos_ref, sin_ref, q_ref, qmem_ref, gate_ref, xn_ref,
                  *, q_tiles, scale, n_grp, grp_gates):
    j = pl.program_id(1)

    @pl.when(j == 0)
    def _():
        xn_ref[...] = _rms(x_ref[...], g_ref[...]).astype(BF16)

    @pl.when(j < q_tiles)
    def _():
        acc = _dot_nt(xn_ref[...], wq_ref[...].astype(BF16))
        _store_heads(q_ref, acc * scale, (cos_ref, sin_ref))

    @pl.when(j == q_tiles)
    def _():
        md = qmem_ref.shape[1]
        acc = _dot_nt(xn_ref[...], wt_ref[...].astype(BF16))
        qmem_ref[...] = acc[:, :md].astype(qmem_ref.dtype)
        gates = jax.nn.sigmoid(acc[:, md:])
        for gi in range(n_grp):
            shift = (LANES - gi * grp_gates) % LANES
            gate_ref[gi] = gates if shift == 0 else pltpu.roll(gates, shift, axis=1)


def _mix_b(x, g, w_t, w_tail, tables, layer, *, q_dim, mem_dim, scale, n_grp, grp_gates, tm=1024, tn=512):
    s, d = x.shape
    q_tiles = q_dim // tn
    hp = tn // HEAD_DIM
    tm = min(tm, s)
    return pl.pallas_call(
        functools.partial(_mix_b_kernel, q_tiles=q_tiles, scale=scale, n_grp=n_grp, grp_gates=grp_gates),
        out_shape=(jax.ShapeDtypeStruct((q_dim // HEAD_DIM, s, HEAD_DIM), BF16),
                   jax.ShapeDtypeStruct((s, mem_dim), BF16),
                   jax.ShapeDtypeStruct((n_grp, s, LANES), F32)),
        grid=(s // tm, q_tiles + 1),
        in_specs=[
            pl.BlockSpec((tm, d), lambda i, j: (i, 0)),
            pl.BlockSpec((1, d), lambda i, j: (0, 0)),
            pl.BlockSpec((None, tn, d), lambda i, j: (layer, jnp.minimum(j, q_tiles - 1), 0)),
            pl.BlockSpec((mem_dim + LANES, d), lambda i, j: (0, 0), pipeline_mode=pl.Buffered(1)),
            pl.BlockSpec((tm, LANES), lambda i, j: (i, 0)),
            pl.BlockSpec((tm, LANES), lambda i, j: (i, 0)),
        ],
        out_specs=(
            pl.BlockSpec((hp, tm, HEAD_DIM), lambda i, j: (jnp.minimum(j, q_tiles - 1), i, 0)),
            pl.BlockSpec((tm, mem_dim), lambda i, j: (i, 0)),
            pl.BlockSpec((n_grp, tm, LANES), lambda i, j: (0, i, 0)),
        ),
        scratch_shapes=[pltpu.VMEM((tm, d), BF16)],
        compiler_params=_params(("arbitrary", "arbitrary")),
        name="mix_b_in",
    )(x, g[None, :], w_t, w_tail, *tables)


def _mix_out_kernel(h_ref, tok_ref, qmem_ref, k_ref, v_ref, w1_ref, w2_ref, o_ref, memo_ref):
    @pl.when(pl.program_id(1) == 0)
    def _():
        scale = HEAD_DIM ** -0.5
        for h in range(MEM_HEADS):
            sl = slice(h * HEAD_DIM, (h + 1) * HEAD_DIM)
            s = _dot_nt(qmem_ref[:, sl], k_ref[:, sl]) * scale
            e = jnp.exp(s - jnp.max(s, axis=-1, keepdims=True))
            o = _dot(e.astype(BF16), v_ref[:, sl]) * (1.0 / jnp.sum(e, axis=-1, keepdims=True))
            memo_ref[:, sl] = o.astype(memo_ref.dtype)

    o_ref[...] = (h_ref[...] + _dot(tok_ref[...], w1_ref[...].astype(BF16))
                  + _dot(memo_ref[...], w2_ref[...].astype(BF16)))


def _mix_out(h, tok, q_mem, mem_kv, w_out, layer, *, tm=2048, tn=512):
    s, d = h.shape
    n_tok = tok.shape[1]
    md = q_mem.shape[1]
    n_mem = mem_kv.shape[0]
    assert n_tok % md == 0
    tm = min(tm, s)
    return pl.pallas_call(
        _mix_out_kernel,
        out_shape=jax.ShapeDtypeStruct((s, d), F32),
        grid=(s // tm, d // tn),
        in_specs=[
            pl.BlockSpec((tm, tn), lambda i, j: (i, j)),
            pl.BlockSpec((tm, n_tok), lambda i, j: (i, 0)),
            pl.BlockSpec((tm, md), lambda i, j: (i, 0)),
            pl.BlockSpec((n_mem, md), lambda i, j: (0, 0)),
            pl.BlockSpec((n_mem, md), lambda i, j: (0, 1)),
            pl.BlockSpec((None, n_tok, tn), lambda i, j: (layer, 0, j)),
            pl.BlockSpec((None, md, tn), lambda i, j: (layer, n_tok // md, j)),
        ],
        out_specs=pl.BlockSpec((tm, tn), lambda i, j: (i, j)),
        scratch_shapes=[pltpu.VMEM((tm, md), BF16)],
        compiler_params=_params(("arbitrary", "arbitrary")),
        name="mix_out",
    )(h, tok, q_mem, mem_kv, mem_kv, w_out, w_out)


def _compress_kernel(c_ref, pos_ref, w1_ref, w2_ref, o_ref):
    x = c_ref[0]
    nc, half = x.shape
    w1a = w1_ref[0, 0:half, :]
    w1b = w1_ref[0, half:, :]
    pos = jnp.broadcast_to(pos_ref[0], (SUBLANES, 2 * half))
    bias = _dot(pos, w1_ref[0])[0:1, :]
    first = _dot(x, w1a)
    second = pltpu.roll(_dot(x, w1b), nc - 1, axis=0)
    hid = jax.nn.gelu(first + second + bias, approximate=True)
    out = _dot(hid.astype(BF16), w2_ref[0])
    row = lax.broadcasted_iota(jnp.int32, out.shape, 0)
    o_ref[0] = jnp.where(row < nc - 1, out, 0.0).astype(o_ref.dtype)


def _compress(chunks, pos, w1, w2):
    n, nc, cw = chunks.shape
    g = n // 2
    return pl.pallas_call(
        _compress_kernel,
        out_shape=jax.ShapeDtypeStruct((2 * g, nc, HEAD_DIM), BF16),
        grid=(2 * g,),
        in_specs=[
            pl.BlockSpec((1, nc, cw), lambda i: (i, 0, 0)),
            pl.BlockSpec((1, 1, 2 * cw), lambda i: (i // g, 0, 0)),
            pl.BlockSpec((1, 2 * cw, HEAD_DIM), lambda i: (i // g, 0, 0)),
            pl.BlockSpec((1, HEAD_DIM, HEAD_DIM), lambda i: (i // g, 0, 0)),
        ],
        out_specs=pl.BlockSpec((1, nc, HEAD_DIM), lambda i: (i, 0, 0)),
        compiler_params=_params(("arbitrary",)),
        name="compress",
    )(chunks, pos, w1, w2)


def _softmax2_parts(s, bias):
    s = s + bias
    e = jnp.exp2(s - jnp.max(s, axis=-1, keepdims=True))
    return e, 1.0 / jnp.sum(e, axis=-1, keepdims=True)


def _nsa_cmp_kernel(q_ref, gate_ref, kc_ref, vc_ref, ovt_ref, kw_ref, vw_ref, ocw_ref, bias_ref,
                    *, tq, tw, rep, n_pick):
    i = pl.program_id(1)
    t0 = i * tq
    rows = rep * tq
    q3 = q_ref[...].reshape(rows, HEAD_DIM)
    gates = gate_ref[...]

    kc = kc_ref[0]
    n_cmp = kc.shape[0]
    cmp_end = lax.broadcasted_iota(jnp.int32, (tq, n_cmp), 1) * CMP_STRIDE + (CMP_BLOCK - 1)
    t_c = t0 + lax.broadcasted_iota(jnp.int32, (tq, n_cmp), 0)
    bias_c = jnp.where(cmp_end <= t_c, 0.0, NEG_BIG)
    sees_any = t0 + lax.broadcasted_iota(jnp.int32, (tq, 1), 0) >= CMP_BLOCK - 1
    o_c, inv_c, p_sum = [], [], None
    for r in range(rep):
        e_r, inv_r = _softmax2_parts(_dot_nt(q_ref[r], kc), bias_c)
        inv_r = jnp.where(sees_any, inv_r, 0.0)
        o_c.append(_dot(e_r.astype(BF16), vc_ref[0]))
        inv_c.append(inv_r)
        p_r = e_r * inv_r
        p_sum = p_r if p_sum is None else p_sum + p_r

    p_hi = p_sum.astype(BF16)
    p_lo = (p_sum - p_hi.astype(F32)).astype(BF16)
    imp_all = _dot_nt(ovt_ref[...], p_hi) + _dot_nt(ovt_ref[...], p_lo)
    n_slc = imp_all.shape[0]
    blk = lax.broadcasted_iota(jnp.int32, (n_slc, LANES), 0)
    blk_f = blk.astype(F32)
    for c in range(tq // LANES):
        imp = imp_all[:, c * LANES:(c + 1) * LANES]
        jt = (t0 + c * LANES + lax.broadcasted_iota(jnp.int32, imp.shape, 1)) // SLC_BLOCK
        valid = blk <= jt
        forced = (blk == 0) | (blk == jt) | (blk == jt - 1)
        work = jnp.where(valid, jnp.where(forced, -jnp.inf, imp), -jnp.inf)
        sel_bias = jnp.where(forced, 0.0, NEG_BIG)
        for _ in range(n_pick):
            mx = jnp.max(work, axis=0, keepdims=True)
            first = jnp.min(jnp.where(work == mx, blk_f, float(n_slc)), axis=0, keepdims=True)
            pick = blk_f == first
            sel_bias = jnp.where(pick, 0.0, sel_bias)
            work = jnp.where(pick, -jnp.inf, work)
        bias_ref[0, c * LANES:(c + 1) * LANES, :] = jnp.where(valid, sel_bias, NEG_BIG).T.astype(bias_ref.dtype)

    wlen = tw + WINDOW
    rel = lax.broadcasted_iota(jnp.int32, (tw, wlen), 1) - lax.broadcasted_iota(jnp.int32, (tw, wlen), 0)
    for sub in range(tq // tw):
        ts0 = t0 + sub * tw
        ws = pl.multiple_of(jnp.maximum(ts0 - WINDOW, 0), tw)
        qs = jnp.concatenate([q_ref[r, sub * tw:(sub + 1) * tw, :] for r in range(rep)], axis=0)
        s_w = _dot_nt(qs, kw_ref[0, pl.ds(ws, wlen), :])
        rel_w = rel + (ws - ts0)
        bias_w = jnp.where((rel_w <= 0) & (rel_w > -WINDOW), 0.0, NEG_BIG)
        e_w, inv_w = _softmax2_parts(s_w, jnp.concatenate([bias_w] * rep, axis=0))
        o_w = _dot(e_w.astype(BF16), vw_ref[0, pl.ds(ws, wlen), :])
        gs = gates[sub * tw:(sub + 1) * tw]
        for r in range(rep):
            cr = slice(sub * tw, (sub + 1) * tw)
            wr = slice(r * tw, (r + 1) * tw)
            ocw_ref[cr, r * HEAD_DIM:(r + 1) * HEAD_DIM] = (
                (gs[:, 3 * r:3 * r + 1] * inv_c[r][cr]) * o_c[r][cr]
                + (gs[:, 3 * r + 2:3 * r + 3] * inv_w[wr]) * o_w[wr])


def _nsa_cmp(q, gates, cmp_kv, overlap, kw, vw, *, tq=512, tw=128):
    n_h, s, _ = q.shape
    g = NSA_KV_GROUPS
    rep = n_h // g
    nc = cmp_kv.shape[1]
    n_slc = s // SLC_BLOCK
    tq = min(tq, s)
    n_pick = max(min(N_SELECT, n_slc) - N_FORCED, 0)
    kernel = functools.partial(_nsa_cmp_kernel, tq=tq, tw=tw, rep=rep, n_pick=n_pick)
    return pl.pallas_call(
        kernel,
        out_shape=(jax.ShapeDtypeStruct((s, n_h * HEAD_DIM), F32), jax.ShapeDtypeStruct((g, s, n_slc), BF16)),
        grid=(g, s // tq),
        in_specs=[
            pl.BlockSpec((rep, tq, HEAD_DIM), lambda gi, i: (gi, i, 0)),
            pl.BlockSpec((None, tq, LANES), lambda gi, i: (gi, i, 0)),
            pl.BlockSpec((1, nc, HEAD_DIM), lambda gi, i: (gi, 0, 0)),
            pl.BlockSpec((1, nc, HEAD_DIM), lambda gi, i: (g + gi, 0, 0)),
            pl.BlockSpec((n_slc, nc), lambda gi, i: (0, 0)),
            pl.BlockSpec((1, s, HEAD_DIM), lambda gi, i: (gi, 0, 0)),
            pl.BlockSpec((1, s, HEAD_DIM), lambda gi, i: (gi, 0, 0)),
        ],
        out_specs=(pl.BlockSpec((tq, rep * HEAD_DIM), lambda gi, i: (i, gi)),
                   pl.BlockSpec((1, tq, n_slc), lambda gi, i: (gi, i, 0))),
        compiler_params=_params(("arbitrary", "arbitrary")),
        name="nsa_cmp_win",
    )(q, gates, cmp_kv, cmp_kv, overlap, kw, vw)


def _nsa_sel_kernel(q_ref, bias_ref, ks_ref, vs_ref, ocw_ref, gate_ref, o_ref, m_ref, acc_ref, sa_ref, sb_ref,
                    *, tq, tk, rep):
    i = pl.program_id(1)
    t0 = i * tq
    rows = rep * tq
    q3 = q_ref[...].reshape(rows, HEAD_DIM)
    q_aug = jnp.concatenate([q3, jnp.concatenate([bias_ref[0]] * rep, axis=0)], axis=1)

    m_ref[...] = jnp.full(m_ref.shape, NEG_BIG, F32)
    acc_ref[...] = jnp.zeros(acc_ref.shape, F32)

    def scores(kt, s_ref):
        k_aug = ks_ref[0, pl.ds(pl.multiple_of(kt * tk, tk), tk), :]
        for r in range(rep):
            rs = slice(r * tq, (r + 1) * tq)
            s_ref[rs, :] = _dot_nt(q_aug[rs], k_aug)

    def consume(kt, s_ref, causal):
        k0 = pl.multiple_of(kt * tk, tk)
        v_aug = vs_ref[0, pl.ds(k0, tk), :]
        for r in range(rep):
            rs = slice(r * tq, (r + 1) * tq)
            s = s_ref[rs, :]
            if causal:
                t_s = t0 + lax.broadcasted_iota(jnp.int32, s.shape, 0)
                s = jnp.where(lax.broadcasted_iota(jnp.int32, s.shape, 1) + k0 <= t_s, s, NEG_BIG)
            m_prev = m_ref[rs, :]
            m_new = jnp.maximum(m_prev, jnp.max(s, axis=-1, keepdims=True))
            alpha = jnp.exp2(m_prev - m_new)
            p = jnp.exp2(s - jnp.concatenate([m_new] * (tk // LANES), axis=1))
            acc_ref[rs, :] = (jnp.concatenate([alpha, alpha], axis=1) * acc_ref[rs, :]
                              + _dot(p.astype(BF16), v_aug))
            m_ref[rs, :] = m_new

    last = (t0 + tq + tk - 1) // tk - 1
    n_pairs = last // 2

    scores(0, sa_ref)

    def pair(a, carry):
        kt = 2 * a
        scores(kt + 1, sb_ref)
        consume(kt, sa_ref, False)
        scores(kt + 2, sa_ref)
        consume(kt + 1, sb_ref, False)
        return carry

    lax.fori_loop(0, n_pairs, pair, 0)

    @pl.when(last == 2 * n_pairs)
    def _():
        consume(last, sa_ref, True)

    @pl.when(last != 2 * n_pairs)
    def _():
        scores(last, sb_ref)
        consume(last - 1, sa_ref, False)
        consume(last, sb_ref, True)

    acc = acc_ref[...]
    o_s = acc[:, :HEAD_DIM] / acc[:, HEAD_DIM:]
    gates = gate_ref[...]
    for r in range(rep):
        cs = slice(r * HEAD_DIM, (r + 1) * HEAD_DIM)
        o_ref[:, cs] = (ocw_ref[:, cs] + gates[:, 3 * r + 1:3 * r + 2] * o_s[r * tq:(r + 1) * tq]).astype(o_ref.dtype)


def _nsa_sel(q, gates, sel_bias, ks_aug, vs_aug, ocw, *, tq=512, tk=512):
    n_h, s, _ = q.shape
    g = NSA_KV_GROUPS
    rep = n_h // g
    n_slc = sel_bias.shape[2]
    tk = min(tk, s)
    tq = min(tq, tk)
    rows = rep * tq
    kernel = functools.partial(_nsa_sel_kernel, tq=tq, tk=tk, rep=rep)
    return pl.pallas_call(
        kernel,
        out_shape=jax.ShapeDtypeStruct((s, n_h * HEAD_DIM), BF16),
        grid=(g, s // tq),
        in_specs=[
            pl.BlockSpec((rep, tq, HEAD_DIM), lambda gi, i: (gi, i, 0)),
            pl.BlockSpec((1, tq, n_slc), lambda gi, i: (gi, i, 0)),
            pl.BlockSpec((1, s, HEAD_DIM + n_slc), lambda gi, i: (gi, 0, 0)),
            pl.BlockSpec((1, s, 2 * HEAD_DIM), lambda gi, i: (gi, 0, 0)),
            pl.BlockSpec((tq, rep * HEAD_DIM), lambda gi, i: (i, gi)),
            pl.BlockSpec((None, tq, LANES), lambda gi, i: (gi, i, 0)),
        ],
        out_specs=pl.BlockSpec((tq, rep * HEAD_DIM), lambda gi, i: (i, gi)),
        scratch_shapes=[pltpu.VMEM((rows, LANES), F32), pltpu.VMEM((rows, 2 * HEAD_DIM), F32),
                        pltpu.VMEM((rows, tk), F32), pltpu.VMEM((rows, tk), F32)],
        compiler_params=_params(("arbitrary", "arbitrary")),
        name="nsa_sel",
    )(q, sel_bias, ks_aug, vs_aug, ocw, gates)


def _slc_from_cmp(n_slc, n_cmp):
    cs = jnp.arange(n_cmp)[None, :] * CMP_STRIDE
    ss = jnp.arange(n_slc)[:, None] * SLC_BLOCK
    ov = jnp.clip(jnp.minimum(cs + CMP_BLOCK, ss + SLC_BLOCK) - jnp.maximum(cs, ss), 0, None)
    return (ov.astype(F32) / CMP_BLOCK).astype(BF16)


def kernel(x, mem, positions, ffn_norm, ffn_w_gate, ffn_w_up, ffn_w_down, mix_norm, mem_norm, w_mem_kv, w_out, w_in_conv, conv_w, w_in_nsa, kv_norm, w_kv, cmp_pos_k, cmp_w1_k, cmp_w2_k, cmp_pos_v, cmp_w1_v, cmp_w2_v, final_norm):
    b, s, d = x.shape
    assert b == 1
    depth = ffn_norm.shape[0]
    n_a = w_in_conv.shape[0]
    mem_dim = w_mem_kv.shape[2] // 2
    conv_ch = conv_w.shape[2]
    kv_w = w_kv.shape[1] // 6
    g = NSA_KV_GROUPS
    n_q = w_in_nsa.shape[2] - mem_dim
    n_heads = n_q // (HEAD_DIM + 3)
    q_dim = n_heads * HEAD_DIM
    rep = n_heads // g
    n_slc = s // SLC_BLOCK
    bf = lambda w: w.astype(BF16)

    h = x[0]
    mem2 = mem[0]
    tables = _rope_tables(positions.reshape(s, 1))

    kv_side = None
    for layer in range(depth):
        if layer == n_a:
            chunks, ks_aug, vs_aug, kw, vw = _kv_side(h, kv_norm, w_kv, tables)
            nc = s // CMP_STRIDE
            pos = bf(jnp.stack([cmp_pos_k, cmp_pos_v]).reshape(2, 1, CMP_BLOCK * HEAD_DIM))
            cmp_kv = _compress(chunks, pos, bf(jnp.stack([cmp_w1_k, cmp_w1_v])), bf(jnp.stack([cmp_w2_k, cmp_w2_v])))
            kv_side = (cmp_kv, ks_aug, vs_aug, kw, vw, _slc_from_cmp(n_slc, nc))

        h = _ffn(h, ffn_norm[layer, 0], ffn_w_gate, ffn_w_up, ffn_w_down, layer, 0)

        if layer < n_a:
            tok, q_mem = _mix_a(h, mix_norm[layer], w_in_conv, conv_w, layer)
        else:
            w_t = jnp.transpose(w_in_nsa, (0, 2, 1))
            w_tail = _nsa_tail(w_t, layer - n_a, q_dim=q_dim, n_gate=3 * n_heads, mem_dim=mem_dim)
            q, q_mem, gates = _mix_b(h, mix_norm[layer], w_t, w_tail, tables, layer - n_a, q_dim=q_dim,
                                     mem_dim=mem_dim, scale=HEAD_DIM ** -0.5 * math.log2(math.e),
                                     n_grp=g, grp_gates=3 * rep)
            cmp_kv, ks_aug, vs_aug, kw, vw, overlap = kv_side
            ocw, sel_bias = _nsa_cmp(q, gates, cmp_kv, overlap, kw, vw)
            tok = _nsa_sel(q, gates, sel_bias, ks_aug, vs_aug, ocw)

        h = _mix_out(h, tok, q_mem, _mem_kv(mem2, mem_norm[layer], w_mem_kv, layer), w_out, layer)

        last = layer == depth - 1
        h = _ffn(h, ffn_norm[layer, 1], ffn_w_gate, ffn_w_up, ffn_w_down, layer, 1, final_norm if last else None)

    return h[None]
```

```python
import functools
import math

import jax
import jax.numpy as jnp
from jax import lax
from jax.experimental import pallas as pl
from jax.experimental.pallas import tpu as pltpu

HEAD_DIM = 128
ROT_DIM = HEAD_DIM // 4
ROPE_THETA = 500000.0
MEM_HEADS = 4
NSA_KV_GROUPS = 4
CMP_STRIDE = 16
CMP_BLOCK = 2 * CMP_STRIDE
SLC_BLOCK = 64
N_SELECT = 16
N_FORCED = 3
WINDOW = 512
RMS_EPS = 1e-6
CONV_K = 3

LANES = 128
SUBLANES = 8
NEG_BIG = -1e30
VMEM_LIMIT = 56 * 1024 * 1024

F32 = jnp.float32
BF16 = jnp.bfloat16


def _dot(a, b):
    return jnp.dot(a, b, preferred_element_type=F32)


def _dot_nt(a, b):
    return lax.dot_general(a, b, (((1,), (1,)), ((), ())), preferred_element_type=F32)


def _rms(x, g):
    ms = jnp.mean(x * x, axis=-1, keepdims=True)
    return x * lax.rsqrt(ms + RMS_EPS) * g


def _params(sem):
    return pltpu.CompilerParams(dimension_semantics=sem, vmem_limit_bytes=VMEM_LIMIT)


def _rope_table_kernel(pos_ref, inv_ref, cos_ref, sin_ref):
    ang = pos_ref[...].astype(F32) * inv_ref[...]
    lane = lax.broadcasted_iota(jnp.int32, ang.shape, 1)
    c = jnp.cos(ang)
    s = jnp.sin(ang)
    half = ROT_DIM // 2
    cos_ref[...] = jnp.where(lane < ROT_DIM, c, 1.0)
    sin_ref[...] = jnp.where(lane < half, -s, jnp.where(lane < ROT_DIM, s, 0.0))


def _rope_tables(positions_col):
    s = positions_col.shape[0]
    half = ROT_DIM // 2
    inv = 1.0 / (ROPE_THETA ** (jnp.arange(half, dtype=F32) / half))
    inv_row = jnp.concatenate([inv, inv, jnp.zeros((LANES - ROT_DIM,), F32)])[None, :]
    tm = min(s, 1024)
    return pl.pallas_call(
        _rope_table_kernel,
        out_shape=(jax.ShapeDtypeStruct((s, LANES), F32), jax.ShapeDtypeStruct((s, LANES), F32)),
        grid=(s // tm,),
        in_specs=[pl.BlockSpec((tm, 1), lambda i: (i, 0)), pl.BlockSpec((1, LANES), lambda i: (0, 0))],
        out_specs=(pl.BlockSpec((tm, LANES), lambda i: (i, 0)), pl.BlockSpec((tm, LANES), lambda i: (i, 0))),
        compiler_params=_params(("arbitrary",)),
        name="rope_tables",
    )(positions_col, inv_row)


def _rotate(x, cos_t, sin_t):
    half = ROT_DIM // 2
    lane = lax.broadcasted_iota(jnp.int32, x.shape, 1)
    swapped = jnp.where(lane < half, pltpu.roll(x, LANES - half, axis=1), pltpu.roll(x, half, axis=1))
    return x * cos_t + swapped * sin_t


def _store_heads(o_ref, acc, tables):
    for h in range(acc.shape[1] // HEAD_DIM):
        xh = acc[:, h * HEAD_DIM:(h + 1) * HEAD_DIM]
        if tables is not None:
            xh = _rotate(xh, tables[0][...], tables[1][...])
        o_ref[h] = xh.astype(o_ref.dtype)


def _ffn_kernel(x_hbm, g_ref, wg_ref, wu_ref, wd_ref, *rest, final):
    if final:
        fg_ref, o_ref, xn_ref, xbuf_ref, x_sem = rest
    else:
        o_ref, xn_ref, xbuf_ref, x_sem = rest
    i = pl.program_id(0)
    j = pl.program_id(1)
    tm = xbuf_ref.shape[0]

    def x_copy(row_tile):
        return pltpu.make_async_copy(x_hbm.at[pl.ds(row_tile * tm, tm), :], xbuf_ref, x_sem)

    @pl.when((i == 0) & (j == 0))
    def _():
        x_copy(0).start()

    @pl.when(j == 0)
    def _():
        x_copy(i).wait()
        x = xbuf_ref[...]
        xn_ref[...] = _rms(x, g_ref[...]).astype(BF16)
        o_ref[...] = x

    @pl.when((j == 1) & (i + 1 < pl.num_programs(0)))
    def _():
        x_copy(i + 1).start()

    xn = xn_ref[...]
    a = _dot(xn, wg_ref[...].astype(BF16))
    b = _dot(xn, wu_ref[...].astype(BF16))
    mid = (a * jax.nn.sigmoid(a) * b).astype(BF16)
    o_ref[...] += 0.5 * _dot(mid, wd_ref[...].astype(BF16))

    if final:
        @pl.when(j == pl.num_programs(1) - 1)
        def _():
            o_ref[...] = _rms(o_ref[...], fg_ref[...])


def _ffn(x, g, w_gate, w_up, w_down, layer, half, final_g=None, *, tm=1024, tf=256):
    s, d = x.shape
    f = w_gate.shape[3]
    tm = min(tm, s)
    assert f // tf >= 2
    final = final_g is not None
    in_specs = [
        pl.BlockSpec(memory_space=pl.ANY),
        pl.BlockSpec((1, d), lambda i, j: (0, 0)),
        pl.BlockSpec((None, None, d, tf), lambda i, j: (layer, half, 0, j)),
        pl.BlockSpec((None, None, d, tf), lambda i, j: (layer, half, 0, j)),
        pl.BlockSpec((None, None, tf, d), lambda i, j: (layer, half, j, 0)),
    ]
    args = [x, g[None, :], w_gate, w_up, w_down]
    if final:
        in_specs.append(pl.BlockSpec((1, d), lambda i, j: (0, 0)))
        args.append(final_g[None, :])
    return pl.pallas_call(
        functools.partial(_ffn_kernel, final=final),
        out_shape=jax.ShapeDtypeStruct((s, d), F32),
        grid=(s // tm, f // tf),
        in_specs=in_specs,
        out_specs=pl.BlockSpec((tm, d), lambda i, j: (i, 0)),
        scratch_shapes=[pltpu.VMEM((tm, d), BF16), pltpu.VMEM((tm, d), F32), pltpu.SemaphoreType.DMA],
        compiler_params=_params(("arbitrary", "arbitrary")),
        name="ffn_final" if final else "ffn",
    )(*args)


def _mem_kv_kernel(x_ref, g_ref, w_ref, o_ref):
    xn = _rms(x_ref[...], g_ref[...]).astype(BF16)
    o_ref[...] = _dot(xn, w_ref[...].astype(BF16)).astype(o_ref.dtype)


def _mem_kv(mem2, g, w_mem_kv, layer, *, tn=512):
    n_mem, d = mem2.shape
    n = w_mem_kv.shape[2]
    return pl.pallas_call(
        _mem_kv_kernel,
        out_shape=jax.ShapeDtypeStruct((n_mem, n), BF16),
        grid=(n // tn,),
        in_specs=[
            pl.BlockSpec((n_mem, d), lambda j: (0, 0)),
            pl.BlockSpec((1, d), lambda j: (0, 0)),
            pl.BlockSpec((None, d, tn), lambda j: (layer, 0, j)),
        ],
        out_specs=pl.BlockSpec((n_mem, tn), lambda j: (0, j)),
        compiler_params=_params(("arbitrary",)),
        name="mem_kv",
    )(mem2, g[None, :], w_mem_kv)


def _kv_side_kernel(x_ref, g_ref, w_ref, cos_ref, sin_ref, chunk_ref, ks_ref, vs_ref, kw_ref, vw_ref,
                    xn_ref, stage_ref, *, n_grp):
    i = pl.program_id(0)
    j = pl.program_id(1)
    tm = x_ref.shape[0]

    @pl.when(j == 0)
    def _():
        xn_ref[...] = _rms(x_ref[...], g_ref[...]).astype(BF16)

    acc = _dot(xn_ref[...], w_ref[...].astype(BF16))

    def head(h, rot):
        xh = acc[:, h * HEAD_DIM:(h + 1) * HEAD_DIM]
        return _rotate(xh, cos_ref[...], sin_ref[...]) if rot else xh

    def store_chunks(first, rot):
        for h in range(n_grp):
            stage_ref[...] = head(h, rot)
            for p in range(CMP_STRIDE):
                chunk_ref[first + h, :, p * HEAD_DIM:(p + 1) * HEAD_DIM] = (
                    stage_ref[pl.ds(p, tm // CMP_STRIDE, stride=CMP_STRIDE), :].astype(chunk_ref.dtype))

    @pl.when(j == 0)
    def _():
        store_chunks(0, True)

    @pl.when(j == 1)
    def _():
        store_chunks(n_grp, False)

    @pl.when(j == 2)
    def _():
        n_slc = ks_ref.shape[2] - HEAD_DIM
        row_blk = (i * tm + lax.broadcasted_iota(jnp.int32, (tm, n_slc), 0)) // SLC_BLOCK
        code = jnp.where(row_blk == lax.broadcasted_iota(jnp.int32, (tm, n_slc), 1), 1.0, 0.0).astype(ks_ref.dtype)
        for h in range(n_grp):
            ks_ref[h, :, 0:HEAD_DIM] = head(h, True).astype(ks_ref.dtype)
            ks_ref[h, :, HEAD_DIM:] = code

    @pl.when(j == 3)
    def _():
        for h in range(n_grp):
            vs_ref[h, :, 0:HEAD_DIM] = head(h, False).astype(vs_ref.dtype)
            vs_ref[h, :, HEAD_DIM:] = jnp.ones((tm, HEAD_DIM), vs_ref.dtype)

    @pl.when(j == 4)
    def _():
        for h in range(n_grp):
            kw_ref[h] = head(h, True).astype(kw_ref.dtype)

    @pl.when(j == 5)
    def _():
        for h in range(n_grp):
            vw_ref[h] = head(h, False).astype(vw_ref.dtype)


def _kv_side(x, g, w_kv, tables, *, tm=1024):
    s, d = x.shape
    grp = NSA_KV_GROUPS
    tn = grp * HEAD_DIM
    assert w_kv.shape[1] == 6 * tn
    tm = min(tm, s)
    n_slc = s // SLC_BLOCK
    nc = s // CMP_STRIDE
    cw = CMP_STRIDE * HEAD_DIM
    row = lambda i, j: (0, i, 0)
    return pl.pallas_call(
        functools.partial(_kv_side_kernel, n_grp=grp),
        out_shape=(jax.ShapeDtypeStruct((2 * grp, nc, cw), BF16),
                   jax.ShapeDtypeStruct((grp, s, HEAD_DIM + n_slc), BF16),
                   jax.ShapeDtypeStruct((grp, s, 2 * HEAD_DIM), BF16),
                   jax.ShapeDtypeStruct((grp, s, HEAD_DIM), BF16),
                   jax.ShapeDtypeStruct((grp, s, HEAD_DIM), BF16)),
        grid=(s // tm, 6),
        in_specs=[
            pl.BlockSpec((tm, d), lambda i, j: (i, 0)),
            pl.BlockSpec((1, d), lambda i, j: (0, 0)),
            pl.BlockSpec((d, tn), lambda i, j: (0, j)),
            pl.BlockSpec((tm, LANES), lambda i, j: (i, 0)),
            pl.BlockSpec((tm, LANES), lambda i, j: (i, 0)),
        ],
        out_specs=(pl.BlockSpec((2 * grp, tm // CMP_STRIDE, cw), row),
                   pl.BlockSpec((grp, tm, HEAD_DIM + n_slc), row),
                   pl.BlockSpec((grp, tm, 2 * HEAD_DIM), row),
                   pl.BlockSpec((grp, tm, HEAD_DIM), row),
                   pl.BlockSpec((grp, tm, HEAD_DIM), row)),
        scratch_shapes=[pltpu.VMEM((tm, d), BF16), pltpu.VMEM((tm, HEAD_DIM), F32)],
        compiler_params=_params(("arbitrary", "arbitrary")),
        name="kv_side",
    )(x, g[None, :], w_kv, *tables)


def _mix_a_kernel(x_ref, g_ref, wb_ref, wc_ref, wv_ref, wq_ref, cw_ref, tok_ref, qmem_ref,
                  xn_ref, halo_ref, ext_ref):
    i = pl.program_id(0)
    j = pl.program_id(1)
    tm = x_ref.shape[0]
    tc = tok_ref.shape[1]

    @pl.when(j == 0)
    def _():
        xn_ref[...] = _rms(x_ref[...], g_ref[...]).astype(BF16)
        qmem_ref[...] = _dot(xn_ref[...], wq_ref[...].astype(BF16)).astype(qmem_ref.dtype)

    xn = xn_ref[...]
    gate_b = _dot(xn, wb_ref[...].astype(BF16))
    p = _dot(xn, wc_ref[...].astype(BF16)) * _dot(xn, wv_ref[...].astype(BF16))

    @pl.when(i == 0)
    def _():
        halo_ref[j] = jnp.zeros((SUBLANES, tc), F32)

    ext_ref[0:SUBLANES, :] = halo_ref[j]
    ext_ref[SUBLANES:, :] = p
    halo_ref[j] = p[tm - SUBLANES:, :]
    y = cw_ref[CONV_K - 1:CONV_K, :] * p
    for k in range(CONV_K - 1):
        off = SUBLANES - (CONV_K - 1 - k)
        y = y + cw_ref[k:k + 1, :] * ext_ref[off:off + tm, :]
    tok_ref[...] = (gate_b * y).astype(tok_ref.dtype)


def _mix_a(x, g, w_in_conv, conv_w, layer, *, tm=1024, tc=256):
    s, d = x.shape
    n_ch = conv_w.shape[2]
    nt = n_ch // tc
    md = w_in_conv.shape[2] - 3 * n_ch
    assert (3 * n_ch) % md == 0
    tm = min(tm, s)
    w_tile = lambda part: pl.BlockSpec((None, d, tc), lambda i, j: (layer, 0, part * nt + j))
    return pl.pallas_call(
        _mix_a_kernel,
        out_shape=(jax.ShapeDtypeStruct((s, n_ch), BF16), jax.ShapeDtypeStruct((s, md), BF16)),
        grid=(s // tm, nt),
        in_specs=[
            pl.BlockSpec((tm, d), lambda i, j: (i, 0)),
            pl.BlockSpec((1, d), lambda i, j: (0, 0)),
            w_tile(0), w_tile(1), w_tile(2),
            pl.BlockSpec((None, d, md), lambda i, j: (layer, 0, 3 * n_ch // md), pipeline_mode=pl.Buffered(1)),
            pl.BlockSpec((None, CONV_K, tc), lambda i, j: (layer, 0, j)),
        ],
        out_specs=(pl.BlockSpec((tm, tc), lambda i, j: (i, j)), pl.BlockSpec((tm, md), lambda i, j: (i, 0))),
        scratch_shapes=[pltpu.VMEM((tm, d), BF16), pltpu.VMEM((nt, SUBLANES, tc), F32),
                        pltpu.VMEM((tm + SUBLANES, tc), F32)],
        compiler_params=_params(("arbitrary", "arbitrary")),
        name="mix_a_in",
    )(x, g[None, :], w_in_conv, w_in_conv, w_in_conv, w_in_conv, conv_w)


def _nsa_tail_kernel(w_ref, o_ref, *, q_dim, n_gate, mem_dim):
    o_ref[0:mem_dim, :] = w_ref[q_dim + n_gate:q_dim + n_gate + mem_dim, :]
    o_ref[mem_dim:mem_dim + n_gate, :] = w_ref[q_dim:q_dim + n_gate, :]
    o_ref[mem_dim + n_gate:, :] = jnp.zeros((LANES - n_gate, o_ref.shape[1]), o_ref.dtype)


def _nsa_tail(w_t, layer, *, q_dim, n_gate, mem_dim, tk=256):
    _, n, d = w_t.shape
    assert n == q_dim + n_gate + mem_dim and n_gate <= LANES
    return pl.pallas_call(
        functools.partial(_nsa_tail_kernel, q_dim=q_dim, n_gate=n_gate, mem_dim=mem_dim),
        out_shape=jax.ShapeDtypeStruct((mem_dim + LANES, d), F32),
        grid=(d // tk,),
        in_specs=[pl.BlockSpec((None, n, tk), lambda i: (layer, 0, i))],
        out_specs=pl.BlockSpec((mem_dim + LANES, tk), lambda i: (0, i)),
        compiler_params=_params(("arbitrary",)),
        name="nsa_tail_w",
    )(w_t)


def _mix_b_kernel(x_ref, g_ref, wq_ref, wt_ref, cos_ref, sin_ref, q_ref, qmem_ref, gate_ref, xn_ref,
                  *, q_tiles, scale, n_grp, grp_gates):
    j = pl.program_id(1)

    @pl.when(j == 0)
    def _():
        xn_ref[...] = _rms(x_ref[...], g_ref[...]).astype(BF16)

    @pl.when(j < q_tiles)
    def _():
        acc = _dot_nt(xn_ref[...], wq_ref[...].astype(BF16))
        _store_heads(q_ref, acc * scale, (cos_ref, sin_ref))

    @pl.when(j == q_tiles)
    def _():
        md = qmem_ref.shape[1]
        acc = _dot_nt(xn_ref[...], wt_ref[...].astype(BF16))
        qmem_ref[...] = acc[:, :md].astype(qmem_ref.dtype)
        gates = jax.nn.sigmoid(acc[:, md:])
        for gi in range(n_grp):
            shift = (LANES - gi * grp_gates) % LANES
            gate_ref[gi] = gates if shift == 0 else pltpu.roll(gates, shift, axis=1)


def _mix_b(x, g, w_t, w_tail, tables, layer, *, q_dim, mem_dim, scale, n_grp, grp_gates, tm=1024, tn=512):
    s, d = x.shape
    q_tiles = q_dim // tn
    hp = tn // HEAD_DIM
    tm = min(tm, s)
    return pl.pallas_call(
        functools.partial(_mix_b_kernel, q_tiles=q_tiles, scale=scale, n_grp=n_grp, grp_gates=grp_gates),
        out_shape=(jax.ShapeDtypeStruct((q_dim // HEAD_DIM, s, HEAD_DIM), BF16),
                   jax.ShapeDtypeStruct((s, mem_dim), BF16),
                   jax.ShapeDtypeStruct((n_grp, s, LANES), F32)),
        grid=(s // tm, q_tiles + 1),
        in_specs=[
            pl.BlockSpec((tm, d), lambda i, j: (i, 0)),
            pl.BlockSpec((1, d), lambda i, j: (0, 0)),
            pl.BlockSpec((None, tn, d), lambda i, j: (layer, jnp.minimum(j, q_tiles - 1), 0)),
            pl.BlockSpec((mem_dim + LANES, d), lambda i, j: (0, 0), pipeline_mode=pl.Buffered(1)),
            pl.BlockSpec((tm, LANES), lambda i, j: (i, 0)),
            pl.BlockSpec((tm, LANES), lambda i, j: (i, 0)),
        ],
        out_specs=(
            pl.BlockSpec((hp, tm, HEAD_DIM), lambda i, j: (jnp.minimum(j, q_tiles - 1), i, 0)),
            pl.BlockSpec((tm, mem_dim), lambda i, j: (i, 0)),
            pl.BlockSpec((n_grp, tm, LANES), lambda i, j: (0, i, 0)),
        ),
        scratch_shapes=[pltpu.VMEM((tm, d), BF16)],
        compiler_params=_params(("arbitrary", "arbitrary")),
        name="mix_b_in",
    )(x, g[None, :], w_t, w_tail, *tables)


def _mix_out_kernel(h_ref, tok_ref, qmem_ref, k_ref, v_ref, w1_ref, w2_ref, o_ref, memo_ref):
    @pl.when(pl.program_id(1) == 0)
    def _():
        scale = HEAD_DIM ** -0.5
        for h in range(MEM_HEADS):
            sl = slice(h * HEAD_DIM, (h + 1) * HEAD_DIM)
            s = _dot_nt(qmem_ref[:, sl], k_ref[:, sl]) * scale
            e = jnp.exp(s - jnp.max(s, axis=-1, keepdims=True))
            o = _dot(e.astype(BF16), v_ref[:, sl]) * (1.0 / jnp.sum(e, axis=-1, keepdims=True))
            memo_ref[:, sl] = o.astype(memo_ref.dtype)

    o_ref[...] = (h_ref[...] + _dot(tok_ref[...], w1_ref[...].astype(BF16))
                  + _dot(memo_ref[...], w2_ref[...].astype(BF16)))


def _mix_out(h, tok, q_mem, mem_kv, w_out, layer, *, tm=2048, tn=512):
    s, d = h.shape
    n_tok = tok.shape[1]
    md = q_mem.shape[1]
    n_mem = mem_kv.shape[0]
    assert n_tok % md == 0
    tm = min(tm, s)
    return pl.pallas_call(
        _mix_out_kernel,
        out_shape=jax.ShapeDtypeStruct((s, d), F32),
        grid=(s // tm, d // tn),
        in_specs=[
            pl.BlockSpec((tm, tn), lambda i, j: (i, j)),
            pl.BlockSpec((tm, n_tok), lambda i, j: (i, 0)),
            pl.BlockSpec((tm, md), lambda i, j: (i, 0)),
            pl.BlockSpec((n_mem, md), lambda i, j: (0, 0)),
            pl.BlockSpec((n_mem, md), lambda i, j: (0, 1)),
            pl.BlockSpec((None, n_tok, tn), lambda i, j: (layer, 0, j)),
            pl.BlockSpec((None, md, tn), lambda i, j: (layer, n_tok // md, j)),
        ],
        out_specs=pl.BlockSpec((tm, tn), lambda i, j: (i, j)),
        scratch_shapes=[pltpu.VMEM((tm, md), BF16)],
        compiler_params=_params(("arbitrary", "arbitrary")),
        name="mix_out",
    )(h, tok, q_mem, mem_kv, mem_kv, w_out, w_out)


def _compress_kernel(c_ref, pos_ref, w1_ref, w2_ref, o_ref):
    x = c_ref[0]
    nc, half = x.shape
    w1a = w1_ref[0, 0:half, :]
    w1b = w1_ref[0, half:, :]
    pos = jnp.broadcast_to(pos_ref[0], (SUBLANES, 2 * half))
    bias = _dot(pos, w1_ref[0])[0:1, :]
    first = _dot(x, w1a)
    second = pltpu.roll(_dot(x, w1b), nc - 1, axis=0)
    hid = jax.nn.gelu(first + second + bias, approximate=True)
    out = _dot(hid.astype(BF16), w2_ref[0])
    row = lax.broadcasted_iota(jnp.int32, out.shape, 0)
    o_ref[0] = jnp.where(row < nc - 1, out, 0.0).astype(o_ref.dtype)


def _compress(chunks, pos, w1, w2):
    n, nc, cw = chunks.shape
    g = n // 2
    return pl.pallas_call(
        _compress_kernel,
        out_shape=jax.ShapeDtypeStruct((2 * g, nc, HEAD_DIM), BF16),
        grid=(2 * g,),
        in_specs=[
            pl.BlockSpec((1, nc, cw), lambda i: (i, 0, 0)),
            pl.BlockSpec((1, 1, 2 * cw), lambda i: (i // g, 0, 0)),
            pl.BlockSpec((1, 2 * cw, HEAD_DIM), lambda i: (i // g, 0, 0)),
            pl.BlockSpec((1, HEAD_DIM, HEAD_DIM), lambda i: (i // g, 0, 0)),
        ],
        out_specs=pl.BlockSpec((1, nc, HEAD_DIM), lambda i: (i, 0, 0)),
        compiler_params=_params(("arbitrary",)),
        name="compress",
    )(chunks, pos, w1, w2)


def _softmax2_parts(s, bias):
    s = s + bias
    e = jnp.exp2(s - jnp.max(s, axis=-1, keepdims=True))
    return e, 1.0 / jnp.sum(e, axis=-1, keepdims=True)


def _nsa_cmp_kernel(q_ref, gate_ref, kc_ref, vc_ref, ovt_ref, kw_ref, vw_ref, ocw_ref, bias_ref,
                    *, tq, tw, rep, n_pick):
    i = pl.program_id(1)
    t0 = i * tq
    rows = rep * tq
    q3 = q_ref[...].reshape(rows, HEAD_DIM)
    gates = gate_ref[...]

    kc = kc_ref[0]
    n_cmp = kc.shape[0]
    cmp_end = lax.broadcasted_iota(jnp.int32, (tq, n_cmp), 1) * CMP_STRIDE + (CMP_BLOCK - 1)
    t_c = t0 + lax.broadcasted_iota(jnp.int32, (tq, n_cmp), 0)
    bias_c = jnp.where(cmp_end <= t_c, 0.0, NEG_BIG)
    sees_any = t0 + lax.broadcasted_iota(jnp.int32, (tq, 1), 0) >= CMP_BLOCK - 1
    o_c, inv_c, p_sum = [], [], None
    for r in range(rep):
        e_r, inv_r = _softmax2_parts(_dot_nt(q_ref[r], kc), bias_c)
        inv_r = jnp.where(sees_any, inv_r, 0.0)
        o_c.append(_dot(e_r.astype(BF16), vc_ref[0]))
        inv_c.append(inv_r)
        p_r = e_r * inv_r
        p_sum = p_r if p_sum is None else p_sum + p_r

    p_hi = p_sum.astype(BF16)
    p_lo = (p_sum - p_hi.astype(F32)).astype(BF16)
    imp_all = _dot_nt(ovt_ref[...], p_hi) + _dot_nt(ovt_ref[...], p_lo)
    n_slc = imp_all.shape[0]
    blk = lax.broadcasted_iota(jnp.int32, (n_slc, LANES), 0)
    blk_f = blk.astype(F32)
    for c in range(tq // LANES):
        imp = imp_all[:, c * LANES:(c + 1) * LANES]
        jt = (t0 + c * LANES + lax.broadcasted_iota(jnp.int32, imp.shape, 1)) // SLC_BLOCK
        valid = blk <= jt
        forced = (blk == 0) | (blk == jt) | (blk == jt - 1)
        work = jnp.where(valid, jnp.where(forced, -jnp.inf, imp), -jnp.inf)
        sel_bias = jnp.where(forced, 0.0, NEG_BIG)
        for _ in range(n_pick):
            mx = jnp.max(work, axis=0, keepdims=True)
            first = jnp.min(jnp.where(work == mx, blk_f, float(n_slc)), axis=0, keepdims=True)
            pick = blk_f == first
            sel_bias = jnp.where(pick, 0.0, sel_bias)
            work = jnp.where(pick, -jnp.inf, work)
        bias_ref[0, c * LANES:(c + 1) * LANES, :] = jnp.where(valid, sel_bias, NEG_BIG).T.astype(bias_ref.dtype)

    wlen = tw + WINDOW
    rel = lax.broadcasted_iota(jnp.int32, (tw, wlen), 1) - lax.broadcasted_iota(jnp.int32, (tw, wlen), 0)
    for sub in range(tq // tw):
        ts0 = t0 + sub * tw
        ws = pl.multiple_of(jnp.maximum(ts0 - WINDOW, 0), tw)
        qs = jnp.concatenate([q_ref[r, sub * tw:(sub + 1) * tw, :] for r in range(rep)], axis=0)
        s_w = _dot_nt(qs, kw_ref[0, pl.ds(ws, wlen), :])
        rel_w = rel + (ws - ts0)
        bias_w = jnp.where((rel_w <= 0) & (rel_w > -WINDOW), 0.0, NEG_BIG)
        e_w, inv_w = _softmax2_parts(s_w, jnp.concatenate([bias_w] * rep, axis=0))
        o_w = _dot(e_w.astype(BF16), vw_ref[0, pl.ds(ws, wlen), :])
        gs = gates[sub * tw:(sub + 1) * tw]
        for r in range(rep):
            cr = slice(sub * tw, (sub + 1) * tw)
            wr = slice(r * tw, (r + 1) * tw)
            ocw_ref[cr, r * HEAD_DIM:(r + 1) * HEAD_DIM] = (
                (gs[:, 3 * r:3 * r + 1] * inv_c[r][cr]) * o_c[r][cr]
                + (gs[:, 3 * r + 2:3 * r + 3] * inv_w[wr]) * o_w[wr])


def _nsa_cmp(q, gates, cmp_kv, overlap, kw, vw, *, tq=1024, tw=128):
    n_h, s, _ = q.shape
    g = NSA_KV_GROUPS
    rep = n_h // g
    nc = cmp_kv.shape[1]
    n_slc = s // SLC_BLOCK
    tq = min(tq, s)
    n_pick = max(min(N_SELECT, n_slc) - N_FORCED, 0)
    kernel = functools.partial(_nsa_cmp_kernel, tq=tq, tw=tw, rep=rep, n_pick=n_pick)
    return pl.pallas_call(
        kernel,
        out_shape=(jax.ShapeDtypeStruct((s, n_h * HEAD_DIM), F32), jax.ShapeDtypeStruct((g, s, n_slc), BF16)),
        grid=(g, s // tq),
        in_specs=[
            pl.BlockSpec((rep, tq, HEAD_DIM), lambda gi, i: (gi, i, 0)),
            pl.BlockSpec((None, tq, LANES), lambda gi, i: (gi, i, 0)),
            pl.BlockSpec((1, nc, HEAD_DIM), lambda gi, i: (gi, 0, 0)),
            pl.BlockSpec((1, nc, HEAD_DIM), lambda gi, i: (g + gi, 0, 0)),
            pl.BlockSpec((n_slc, nc), lambda gi, i: (0, 0)),
            pl.BlockSpec((1, s, HEAD_DIM), lambda gi, i: (gi, 0, 0)),
            pl.BlockSpec((1, s, HEAD_DIM), lambda gi, i: (gi, 0, 0)),
        ],
        out_specs=(pl.BlockSpec((tq, rep * HEAD_DIM), lambda gi, i: (i, gi)),
                   pl.BlockSpec((1, tq, n_slc), lambda gi, i: (gi, i, 0))),
        compiler_params=_params(("arbitrary", "arbitrary")),
        name="nsa_cmp_win",
    )(q, gates, cmp_kv, cmp_kv, overlap, kw, vw)


def _nsa_sel_kernel(q_ref, bias_ref, ks_ref, vs_ref, ocw_ref, gate_ref, o_ref, m_ref, acc_ref, sa_ref, sb_ref,
                    *, tq, tk, rep):
    i = pl.program_id(1)
    t0 = i * tq
    rows = rep * tq
    q3 = q_ref[...].reshape(rows, HEAD_DIM)
    q_aug = jnp.concatenate([q3, jnp.concatenate([bias_ref[0]] * rep, axis=0)], axis=1)

    m_ref[...] = jnp.full(m_ref.shape, NEG_BIG, F32)
    acc_ref[...] = jnp.zeros(acc_ref.shape, F32)

    def scores(kt, s_ref):
        k_aug = ks_ref[0, pl.ds(pl.multiple_of(kt * tk, tk), tk), :]
        for r in range(rep):
            rs = slice(r * tq, (r + 1) * tq)
            s_ref[rs, :] = _dot_nt(q_aug[rs], k_aug)

    def consume(kt, s_ref, causal):
        k0 = pl.multiple_of(kt * tk, tk)
        v_aug = vs_ref[0, pl.ds(k0, tk), :]
        for r in range(rep):
            rs = slice(r * tq, (r + 1) * tq)
            s = s_ref[rs, :]
            if causal:
                t_s = t0 + lax.broadcasted_iota(jnp.int32, s.shape, 0)
                s = jnp.where(lax.broadcasted_iota(jnp.int32, s.shape, 1) + k0 <= t_s, s, NEG_BIG)
            m_prev = m_ref[rs, :]
            m_new = jnp.maximum(m_prev, jnp.max(s, axis=-1, keepdims=True))
            alpha = jnp.exp2(m_prev - m_new)
            p = jnp.exp2(s - jnp.concatenate([m_new] * (tk // LANES), axis=1))
            acc_ref[rs, :] = (jnp.concatenate([alpha, alpha], axis=1) * acc_ref[rs, :]
                              + _dot(p.astype(BF16), v_aug))
            m_ref[rs, :] = m_new

    last = (t0 + tq + tk - 1) // tk - 1
    n_pairs = last // 2

    scores(0, sa_ref)

    def pair(a, carry):
        kt = 2 * a
        scores(kt + 1, sb_ref)
        consume(kt, sa_ref, False)
        scores(kt + 2, sa_ref)
        consume(kt + 1, sb_ref, False)
        return carry

    lax.fori_loop(0, n_pairs, pair, 0)

    @pl.when(last == 2 * n_pairs)
    def _():
        consume(last, sa_ref, True)

    @pl.when(last != 2 * n_pairs)
    def _():
        scores(last, sb_ref)
        consume(last - 1, sa_ref, False)
        consume(last, sb_ref, True)

    acc = acc_ref[...]
    o_s = acc[:, :HEAD_DIM] / acc[:, HEAD_DIM:]
    gates = gate_ref[...]
    for r in range(rep):
        cs = slice(r * HEAD_DIM, (r + 1) * HEAD_DIM)
        o_ref[:, cs] = (ocw_ref[:, cs] + gates[:, 3 * r + 1:3 * r + 2] * o_s[r * tq:(r + 1) * tq]).astype(o_ref.dtype)


def _nsa_sel(q, gates, sel_bias, ks_aug, vs_aug, ocw, *, tq=512, tk=512):
    n_h, s, _ = q.shape
    g = NSA_KV_GROUPS
    rep = n_h // g
    n_slc = sel_bias.shape[2]
    tk = min(tk, s)
    tq = min(tq, tk)
    rows = rep * tq
    kernel = functools.partial(_nsa_sel_kernel, tq=tq, tk=tk, rep=rep)
    return pl.pallas_call(
        kernel,
        out_shape=jax.ShapeDtypeStruct((s, n_h * HEAD_DIM), BF16),
        grid=(g, s // tq),
        in_specs=[
            pl.BlockSpec((rep, tq, HEAD_DIM), lambda gi, i: (gi, i, 0)),
            pl.BlockSpec((1, tq, n_slc), lambda gi, i: (gi, i, 0)),
            pl.BlockSpec((1, s, HEAD_DIM + n_slc), lambda gi, i: (gi, 0, 0)),
            pl.BlockSpec((1, s, 2 * HEAD_DIM), lambda gi, i: (gi, 0, 0)),
            pl.BlockSpec((tq, rep * HEAD_DIM), lambda gi, i: (i, gi)),
            pl.BlockSpec((None, tq, LANES), lambda gi, i: (gi, i, 0)),
        ],
        out_specs=pl.BlockSpec((tq, rep * HEAD_DIM), lambda gi, i: (i, gi)),
        scratch_shapes=[pltpu.VMEM((rows, LANES), F32), pltpu.VMEM((rows, 2 * HEAD_DIM), F32),
                        pltpu.VMEM((rows, tk), F32), pltpu.VMEM((rows, tk), F32)],
        compiler_params=_params(("arbitrary", "arbitrary")),
        name="nsa_sel",
    )(q, sel_bias, ks_aug, vs_aug, ocw, gates)


def _slc_from_cmp(n_slc, n_cmp):
    cs = jnp.arange(n_cmp)[None, :] * CMP_STRIDE
    ss = jnp.arange(n_slc)[:, None] * SLC_BLOCK
    ov = jnp.clip(jnp.minimum(cs + CMP_BLOCK, ss + SLC_BLOCK) - jnp.maximum(cs, ss), 0, None)
    return (ov.astype(F32) / CMP_BLOCK).astype(BF16)


def kernel(x, mem, positions, ffn_norm, ffn_w_gate, ffn_w_up, ffn_w_down, mix_norm, mem_norm, w_mem_kv, w_out, w_in_conv, conv_w, w_in_nsa, kv_norm, w_kv, cmp_pos_k, cmp_w1_k, cmp_w2_k, cmp_pos_v, cmp_w1_v, cmp_w2_v, final_norm):
    b, s, d = x.shape
    assert b == 1
    depth = ffn_norm.shape[0]
    n_a = w_in_conv.shape[0]
    mem_dim = w_mem_kv.shape[2] // 2
    conv_ch = conv_w.shape[2]
    kv_w = w_kv.shape[1] // 6
    g = NSA_KV_GROUPS
    n_q = w_in_nsa.shape[2] - mem_dim
    n_heads = n_q // (HEAD_DIM + 3)
    q_dim = n_heads * HEAD_DIM
    rep = n_heads // g
    n_slc = s // SLC_BLOCK
    bf = lambda w: w.astype(BF16)

    h = x[0]
    mem2 = mem[0]
    tables = _rope_tables(positions.reshape(s, 1))

    kv_side = None
    for layer in range(depth):
        if layer == n_a:
            chunks, ks_aug, vs_aug, kw, vw = _kv_side(h, kv_norm, w_kv, tables)
            nc = s // CMP_STRIDE
            pos = bf(jnp.stack([cmp_pos_k, cmp_pos_v]).reshape(2, 1, CMP_BLOCK * HEAD_DIM))
            cmp_kv = _compress(chunks, pos, bf(jnp.stack([cmp_w1_k, cmp_w1_v])), bf(jnp.stack([cmp_w2_k, cmp_w2_v])))
            kv_side = (cmp_kv, ks_aug, vs_aug, kw, vw, _slc_from_cmp(n_slc, nc))

        h = _ffn(h, ffn_norm[layer, 0], ffn_w_gate, ffn_w_up, ffn_w_down, layer, 0)

        if layer < n_a:
            tok, q_mem = _mix_a(h, mix_norm[layer], w_in_conv, conv_w, layer)
        else:
            w_t = jnp.transpose(w_in_nsa, (0, 2, 1))
            w_tail = _nsa_tail(w_t, layer - n_a, q_dim=q_dim, n_gate=3 * n_heads, mem_dim=mem_dim)
            q, q_mem, gates = _mix_b(h, mix_norm[layer], w_t, w_tail, tables, layer - n_a, q_dim=q_dim,
                                     mem_dim=mem_dim, scale=HEAD_DIM ** -0.5 * math.log2(math.e),
                                     n_grp=g, grp_gates=3 * rep)
            cmp_kv, ks_aug, vs_aug, kw, vw, overlap = kv_side
            ocw, sel_bias = _nsa_cmp(q, gates, cmp_kv, overlap, kw, vw)
            tok = _nsa_sel(q, gates, sel_bias, ks_aug, vs_aug, ocw)

        h = _mix_out(h, tok, q_mem, _mem_kv(mem2, mem_norm[layer], w_mem_kv, layer), w_out, layer)

        last = layer == depth - 1
        h = _ffn(h, ffn_norm[layer, 1], ffn_w_gate, ffn_w_up, ffn_w_down, layer, 1, final_norm if last else None)

    return h[None]
```

```python
import functools
import math

import jax
import jax.numpy as jnp
from jax import lax
from jax.experimental import pallas as pl
from jax.experimental.pallas import tpu as pltpu

HEAD_DIM = 128
ROT_DIM = HEAD_DIM // 4
ROPE_THETA = 500000.0
MEM_HEADS = 4
NSA_KV_GROUPS = 4
CMP_STRIDE = 16
CMP_BLOCK = 2 * CMP_STRIDE
SLC_BLOCK = 64
N_SELECT = 16
N_FORCED = 3
WINDOW = 512
RMS_EPS = 1e-6
CONV_K = 3

LANES = 128
SUBLANES = 8
NEG_BIG = -1e30
VMEM_LIMIT = 56 * 1024 * 1024

F32 = jnp.float32
BF16 = jnp.bfloat16


def _dot(a, b):
    return jnp.dot(a, b, preferred_element_type=F32)


def _dot_nt(a, b):
    return lax.dot_general(a, b, (((1,), (1,)), ((), ())), preferred_element_type=F32)


def _rms(x, g):
    ms = jnp.mean(x * x, axis=-1, keepdims=True)
    return x * lax.rsqrt(ms + RMS_EPS) * g


def _params(sem):
    return pltpu.CompilerParams(dimension_semantics=sem, vmem_limit_bytes=VMEM_LIMIT)


def _rope_table_kernel(pos_ref, inv_ref, cos_ref, sin_ref):
    ang = pos_ref[...].astype(F32) * inv_ref[...]
    lane = lax.broadcasted_iota(jnp.int32, ang.shape, 1)
    c = jnp.cos(ang)
    s = jnp.sin(ang)
    half = ROT_DIM // 2
    cos_ref[...] = jnp.where(lane < ROT_DIM, c, 1.0)
    sin_ref[...] = jnp.where(lane < half, -s, jnp.where(lane < ROT_DIM, s, 0.0))


def _rope_tables(positions_col):
    s = positions_col.shape[0]
    half = ROT_DIM // 2
    inv = 1.0 / (ROPE_THETA ** (jnp.arange(half, dtype=F32) / half))
    inv_row = jnp.concatenate([inv, inv, jnp.zeros((LANES - ROT_DIM,), F32)])[None, :]
    tm = min(s, 1024)
    return pl.pallas_call(
        _rope_table_kernel,
        out_shape=(jax.ShapeDtypeStruct((s, LANES), F32), jax.ShapeDtypeStruct((s, LANES), F32)),
        grid=(s // tm,),
        in_specs=[pl.BlockSpec((tm, 1), lambda i: (i, 0)), pl.BlockSpec((1, LANES), lambda i: (0, 0))],
        out_specs=(pl.BlockSpec((tm, LANES), lambda i: (i, 0)), pl.BlockSpec((tm, LANES), lambda i: (i, 0))),
        compiler_params=_params(("arbitrary",)),
        name="rope_tables",
    )(positions_col, inv_row)


def _rotate(x, cos_t, sin_t):
    half = ROT_DIM // 2
    lane = lax.broadcasted_iota(jnp.int32, x.shape, 1)
    swapped = jnp.where(lane < half, pltpu.roll(x, LANES - half, axis=1), pltpu.roll(x, half, axis=1))
    return x * cos_t + swapped * sin_t


def _store_heads(o_ref, acc, tables):
    for h in range(acc.shape[1] // HEAD_DIM):
        xh = acc[:, h * HEAD_DIM:(h + 1) * HEAD_DIM]
        if tables is not None:
            xh = _rotate(xh, tables[0][...], tables[1][...])
        o_ref[h] = xh.astype(o_ref.dtype)


def _ffn_kernel(x_hbm, g_ref, wg_ref, wu_ref, wd_ref, *rest, final):
    if final:
        fg_ref, o_ref, xn_ref, xbuf_ref, x_sem = rest
    else:
        o_ref, xn_ref, xbuf_ref, x_sem = rest
    i = pl.program_id(0)
    j = pl.program_id(1)
    tm = xbuf_ref.shape[0]

    def x_copy(row_tile):
        return pltpu.make_async_copy(x_hbm.at[pl.ds(row_tile * tm, tm), :], xbuf_ref, x_sem)

    @pl.when((i == 0) & (j == 0))
    def _():
        x_copy(0).start()

    @pl.when(j == 0)
    def _():
        x_copy(i).wait()
        x = xbuf_ref[...]
        xn_ref[...] = _rms(x, g_ref[...]).astype(BF16)
        o_ref[...] = x

    @pl.when((j == 1) & (i + 1 < pl.num_programs(0)))
    def _():
        x_copy(i + 1).start()

    xn = xn_ref[...]
    a = _dot(xn, wg_ref[...].astype(BF16))
    b = _dot(xn, wu_ref[...].astype(BF16))
    mid = (a * jax.nn.sigmoid(a) * b).astype(BF16)
    o_ref[...] += 0.5 * _dot(mid, wd_ref[...].astype(BF16))

    if final:
        @pl.when(j == pl.num_programs(1) - 1)
        def _():
            o_ref[...] = _rms(o_ref[...], fg_ref[...])


def _ffn(x, g, w_gate, w_up, w_down, layer, half, final_g=None, *, tm=1024, tf=256):
    s, d = x.shape
    f = w_gate.shape[3]
    tm = min(tm, s)
    assert f // tf >= 2
    final = final_g is not None
    in_specs = [
        pl.BlockSpec(memory_space=pl.ANY),
        pl.BlockSpec((1, d), lambda i, j: (0, 0)),
        pl.BlockSpec((None, None, d, tf), lambda i, j: (layer, half, 0, j)),
        pl.BlockSpec((None, None, d, tf), lambda i, j: (layer, half, 0, j)),
        pl.BlockSpec((None, None, tf, d), lambda i, j: (layer, half, j, 0)),
    ]
    args = [x, g[None, :], w_gate, w_up, w_down]
    if final:
        in_specs.append(pl.BlockSpec((1, d), lambda i, j: (0, 0)))
        args.append(final_g[None, :])
    return pl.pallas_call(
        functools.partial(_ffn_kernel, final=final),
        out_shape=jax.ShapeDtypeStruct((s, d), F32),
        grid=(s // tm, f // tf),
        in_specs=in_specs,
        out_specs=pl.BlockSpec((tm, d), lambda i, j: (i, 0)),
        scratch_shapes=[pltpu.VMEM((tm, d), BF16), pltpu.VMEM((tm, d), F32), pltpu.SemaphoreType.DMA],
        compiler_params=_params(("arbitrary", "arbitrary")),
        name="ffn_final" if final else "ffn",
    )(*args)


def _mem_kv_kernel(x_ref, g_ref, w_ref, o_ref):
    xn = _rms(x_ref[...], g_ref[...]).astype(BF16)
    o_ref[...] = _dot(xn, w_ref[...].astype(BF16)).astype(o_ref.dtype)


def _mem_kv(mem2, g, w_mem_kv, layer, *, tn=512):
    n_mem, d = mem2.shape
    n = w_mem_kv.shape[2]
    return pl.pallas_call(
        _mem_kv_kernel,
        out_shape=jax.ShapeDtypeStruct((n_mem, n), BF16),
        grid=(n // tn,),
        in_specs=[
            pl.BlockSpec((n_mem, d), lambda j: (0, 0)),
            pl.BlockSpec((1, d), lambda j: (0, 0)),
            pl.BlockSpec((None, d, tn), lambda j: (layer, 0, j)),
        ],
        out_specs=pl.BlockSpec((n_mem, tn), lambda j: (0, j)),
        compiler_params=_params(("arbitrary",)),
        name="mem_kv",
    )(mem2, g[None, :], w_mem_kv)


def _kv_side_kernel(x_ref, g_ref, w_ref, cos_ref, sin_ref, chunk_ref, ks_ref, vs_ref, kw_ref, vw_ref,
                    xn_ref, stage_ref, *, n_grp):
    i = pl.program_id(0)
    j = pl.program_id(1)
    tm = x_ref.shape[0]

    @pl.when(j == 0)
    def _():
        xn_ref[...] = _rms(x_ref[...], g_ref[...]).astype(BF16)

    acc = _dot(xn_ref[...], w_ref[...].astype(BF16))

    def head(h, rot):
        xh = acc[:, h * HEAD_DIM:(h + 1) * HEAD_DIM]
        return _rotate(xh, cos_ref[...], sin_ref[...]) if rot else xh

    def store_chunks(first, rot):
        for h in range(n_grp):
            stage_ref[...] = head(h, rot)
            for p in range(CMP_STRIDE):
                chunk_ref[first + h, :, p * HEAD_DIM:(p + 1) * HEAD_DIM] = (
                    stage_ref[pl.ds(p, tm // CMP_STRIDE, stride=CMP_STRIDE), :].astype(chunk_ref.dtype))

    @pl.when(j == 0)
    def _():
        store_chunks(0, True)

    @pl.when(j == 1)
    def _():
        store_chunks(n_grp, False)

    @pl.when(j == 2)
    def _():
        n_slc = ks_ref.shape[2] - HEAD_DIM
        row_blk = (i * tm + lax.broadcasted_iota(jnp.int32, (tm, n_slc), 0)) // SLC_BLOCK
        code = jnp.where(row_blk == lax.broadcasted_iota(jnp.int32, (tm, n_slc), 1), 1.0, 0.0).astype(ks_ref.dtype)
        for h in range(n_grp):
            ks_ref[h, :, 0:HEAD_DIM] = head(h, True).astype(ks_ref.dtype)
            ks_ref[h, :, HEAD_DIM:] = code

    @pl.when(j == 3)
    def _():
        for h in range(n_grp):
            vs_ref[h, :, 0:HEAD_DIM] = head(h, False).astype(vs_ref.dtype)
            vs_ref[h, :, HEAD_DIM:] = jnp.ones((tm, HEAD_DIM), vs_ref.dtype)

    @pl.when(j == 4)
    def _():
        for h in range(n_grp):
            kw_ref[h] = head(h, True).astype(kw_ref.dtype)

    @pl.when(j == 5)
    def _():
        for h in range(n_grp):
            vw_ref[h] = head(h, False).astype(vw_ref.dtype)


def _kv_side(x, g, w_kv, tables, *, tm=1024):
    s, d = x.shape
    grp = NSA_KV_GROUPS
    tn = grp * HEAD_DIM
    assert w_kv.shape[1] == 6 * tn
    tm = min(tm, s)
    n_slc = s // SLC_BLOCK
    nc = s // CMP_STRIDE
    cw = CMP_STRIDE * HEAD_DIM
    row = lambda i, j: (0, i, 0)
    return pl.pallas_call(
        functools.partial(_kv_side_kernel, n_grp=grp),
        out_shape=(jax.ShapeDtypeStruct((2 * grp, nc, cw), BF16),
                   jax.ShapeDtypeStruct((grp, s, HEAD_DIM + n_slc), BF16),
                   jax.ShapeDtypeStruct((grp, s, 2 * HEAD_DIM), BF16),
                   jax.ShapeDtypeStruct((grp, s, HEAD_DIM), BF16),
                   jax.ShapeDtypeStruct((grp, s, HEAD_DIM), BF16)),
        grid=(s // tm, 6),
        in_specs=[
            pl.BlockSpec((tm, d), lambda i, j: (i, 0)),
            pl.BlockSpec((1, d), lambda i, j: (0, 0)),
            pl.BlockSpec((d, tn), lambda i, j: (0, j)),
            pl.BlockSpec((tm, LANES), lambda i, j: (i, 0)),
            pl.BlockSpec((tm, LANES), lambda i, j: (i, 0)),
        ],
        out_specs=(pl.BlockSpec((2 * grp, tm // CMP_STRIDE, cw), row),
                   pl.BlockSpec((grp, tm, HEAD_DIM + n_slc), row),
                   pl.BlockSpec((grp, tm, 2 * HEAD_DIM), row),
                   pl.BlockSpec((grp, tm, HEAD_DIM), row),
                   pl.BlockSpec((grp, tm, HEAD_DIM), row)),
        scratch_shapes=[pltpu.VMEM((tm, d), BF16), pltpu.VMEM((tm, HEAD_DIM), F32)],
        compiler_params=_params(("arbitrary", "arbitrary")),
        name="kv_side",
    )(x, g[None, :], w_kv, *tables)


def _mix_a_kernel(x_ref, g_ref, wb_ref, wc_ref, wv_ref, wq_ref, cw_ref, tok_ref, qmem_ref,
                  xn_ref, halo_ref, ext_ref):
    i = pl.program_id(0)
    j = pl.program_id(1)
    tm = x_ref.shape[0]
    tc = tok_ref.shape[1]

    @pl.when(j == 0)
    def _():
        xn_ref[...] = _rms(x_ref[...], g_ref[...]).astype(BF16)
        qmem_ref[...] = _dot(xn_ref[...], wq_ref[...].astype(BF16)).astype(qmem_ref.dtype)

    xn = xn_ref[...]
    gate_b = _dot(xn, wb_ref[...].astype(BF16))
    p = _dot(xn, wc_ref[...].astype(BF16)) * _dot(xn, wv_ref[...].astype(BF16))

    @pl.when(i == 0)
    def _():
        halo_ref[j] = jnp.zeros((SUBLANES, tc), F32)

    ext_ref[0:SUBLANES, :] = halo_ref[j]
    ext_ref[SUBLANES:, :] = p
    halo_ref[j] = p[tm - SUBLANES:, :]
    y = cw_ref[CONV_K - 1:CONV_K, :] * p
    for k in range(CONV_K - 1):
        off = SUBLANES - (CONV_K - 1 - k)
        y = y + cw_ref[k:k + 1, :] * ext_ref[off:off + tm, :]
    tok_ref[...] = (gate_b * y).astype(tok_ref.dtype)


def _mix_a(x, g, w_in_conv, conv_w, layer, *, tm=1024, tc=256):
    s, d = x.shape
    n_ch = conv_w.shape[2]
    nt = n_ch // tc
    md = w_in_conv.shape[2] - 3 * n_ch
    assert (3 * n_ch) % md == 0
    tm = min(tm, s)
    w_tile = lambda part: pl.BlockSpec((None, d, tc), lambda i, j: (layer, 0, part * nt + j))
    return pl.pallas_call(
        _mix_a_kernel,
        out_shape=(jax.ShapeDtypeStruct((s, n_ch), BF16), jax.ShapeDtypeStruct((s, md), BF16)),
        grid=(s // tm, nt),
        in_specs=[
            pl.BlockSpec((tm, d), lambda i, j: (i, 0)),
            pl.BlockSpec((1, d), lambda i, j: (0, 0)),
            w_tile(0), w_tile(1), w_tile(2),
            pl.BlockSpec((None, d, md), lambda i, j: (layer, 0, 3 * n_ch // md), pipeline_mode=pl.Buffered(1)),
            pl.BlockSpec((None, CONV_K, tc), lambda i, j: (layer, 0, j)),
        ],
        out_specs=(pl.BlockSpec((tm, tc), lambda i, j: (i, j)), pl.BlockSpec((tm, md), lambda i, j: (i, 0))),
        scratch_shapes=[pltpu.VMEM((tm, d), BF16), pltpu.VMEM((nt, SUBLANES, tc), F32),
                        pltpu.VMEM((tm + SUBLANES, tc), F32)],
        compiler_params=_params(("arbitrary", "arbitrary")),
        name="mix_a_in",
    )(x, g[None, :], w_in_conv, w_in_conv, w_in_conv, w_in_conv, conv_w)


def _nsa_tail_kernel(w_ref, o_ref, *, q_dim, n_gate, mem_dim):
    o_ref[0:mem_dim, :] = w_ref[q_dim + n_gate:q_dim + n_gate + mem_dim, :]
    o_ref[mem_dim:mem_dim + n_gate, :] = w_ref[q_dim:q_dim + n_gate, :]
    o_ref[mem_dim + n_gate:, :] = jnp.zeros((LANES - n_gate, o_ref.shape[1]), o_ref.dtype)


def _nsa_tail(w_t, layer, *, q_dim, n_gate, mem_dim, tk=256):
    _, n, d = w_t.shape
    assert n == q_dim + n_gate + mem_dim and n_gate <= LANES
    return pl.pallas_call(
        functools.partial(_nsa_tail_kernel, q_dim=q_dim, n_gate=n_gate, mem_dim=mem_dim),
        out_shape=jax.ShapeDtypeStruct((mem_dim + LANES, d), F32),
        grid=(d // tk,),
        in_specs=[pl.BlockSpec((None, n, tk), lambda i: (layer, 0, i))],
        out_specs=pl.BlockSpec((mem_dim + LANES, tk), lambda i: (0, i)),
        compiler_params=_params(("arbitrary",)),
        name="nsa_tail_w",
    )(w_t)


def _mix_b_kernel(x_ref, g_ref, wq_ref, wt_ref, cos_ref, sin_ref, q_ref, qmem_ref, gate_ref, xn_ref,
                  *, q_tiles, scale, n_grp, grp_gates):
    j = pl.program_id(1)

    @pl.when(j == 0)
    def _():
        xn_ref[...] = _rms(x_ref[...], g_ref[...]).astype(BF16)

    @pl.when(j < q_tiles)
    def _():
        acc = _dot_nt(xn_ref[...], wq_ref[...].astype(BF16))
        _store_heads(q_ref, acc * scale, (cos_ref, sin_ref))

    @pl.when(j == q_tiles)
    def _():
        md = qmem_ref.shape[1]
        acc = _dot_nt(xn_ref[...], wt_ref[...].astype(BF16))
        qmem_ref[...] = acc[:, :md].astype(qmem_ref.dtype)
        gates = jax.nn.sigmoid(acc[:, md:])
        for gi in range(n_grp):
            shift = (LANES - gi * grp_gates) % LANES
            gate_ref[gi] = gates if shift == 0 else pltpu.roll(gates, shift, axis=1)


def _mix_b(x, g, w_t, w_tail, tables, layer, *, q_dim, mem_dim, scale, n_grp, grp_gates, tm=1024, tn=512):
    s, d = x.shape
    q_tiles = q_dim // tn
    hp = tn // HEAD_DIM
    tm = min(tm, s)
    return pl.pallas_call(
        functools.partial(_mix_b_kernel, q_tiles=q_tiles, scale=scale, n_grp=n_grp, grp_gates=grp_gates),
        out_shape=(jax.ShapeDtypeStruct((q_dim // HEAD_DIM, s, HEAD_DIM), BF16),
                   jax.ShapeDtypeStruct((s, mem_dim), BF16),
                   jax.ShapeDtypeStruct((n_grp, s, LANES), F32)),
        grid=(s // tm, q_tiles + 1),
        in_specs=[
            pl.BlockSpec((tm, d), lambda i, j: (i, 0)),
            pl.BlockSpec((1, d), lambda i, j: (0, 0)),
            pl.BlockSpec((None, tn, d), lambda i, j: (layer, jnp.minimum(j, q_tiles - 1), 0)),
            pl.BlockSpec((mem_dim + LANES, d), lambda i, j: (0, 0), pipeline_mode=pl.Buffered(1)),
            pl.BlockSpec((tm, LANES), lambda i, j: (i, 0)),
            pl.BlockSpec((tm, LANES), lambda i, j: (i, 0)),
        ],
        out_specs=(
            pl.BlockSpec((hp, tm, HEAD_DIM), lambda i, j: (jnp.minimum(j, q_tiles - 1), i, 0)),
            pl.BlockSpec((tm, mem_dim), lambda i, j: (i, 0)),
            pl.BlockSpec((n_grp, tm, LANES), lambda i, j: (0, i, 0)),
        ),
        scratch_shapes=[pltpu.VMEM((tm, d), BF16)],
        compiler_params=_params(("arbitrary", "arbitrary")),
        name="mix_b_in",
    )(x, g[None, :], w_t, w_tail, *tables)


def _mix_out_kernel(h_ref, tok_ref, qmem_ref, k_ref, v_ref, w1_ref, w2_ref, o_ref, memo_ref):
    @pl.when(pl.program_id(1) == 0)
    def _():
        scale = HEAD_DIM ** -0.5
        for h in range(MEM_HEADS):
            sl = slice(h * HEAD_DIM, (h + 1) * HEAD_DIM)
            s = _dot_nt(qmem_ref[:, sl], k_ref[:, sl]) * scale
            e = jnp.exp(s - jnp.max(s, axis=-1, keepdims=True))
            o = _dot(e.astype(BF16), v_ref[:, sl]) * (1.0 / jnp.sum(e, axis=-1, keepdims=True))
            memo_ref[:, sl] = o.astype(memo_ref.dtype)

    o_ref[...] = (h_ref[...] + _dot(tok_ref[...], w1_ref[...].astype(BF16))
                  + _dot(memo_ref[...], w2_ref[...].astype(BF16)))


def _mix_out(h, tok, q_mem, mem_kv, w_out, layer, *, tm=2048, tn=512):
    s, d = h.shape
    n_tok = tok.shape[1]
    md = q_mem.shape[1]
    n_mem = mem_kv.shape[0]
    assert n_tok % md == 0
    tm = min(tm, s)
    return pl.pallas_call(
        _mix_out_kernel,
        out_shape=jax.ShapeDtypeStruct((s, d), F32),
        grid=(s // tm, d // tn),
        in_specs=[
            pl.BlockSpec((tm, tn), lambda i, j: (i, j)),
            pl.BlockSpec((tm, n_tok), lambda i, j: (i, 0)),
            pl.BlockSpec((tm, md), lambda i, j: (i, 0)),
            pl.BlockSpec((n_mem, md), lambda i, j: (0, 0)),
            pl.BlockSpec((n_mem, md), lambda i, j: (0, 1)),
            pl.BlockSpec((None, n_tok, tn), lambda i, j: (layer, 0, j)),
            pl.BlockSpec((None, md, tn), lambda i, j: (layer, n_tok // md, j)),
        ],
        out_specs=pl.BlockSpec((tm, tn), lambda i, j: (i, j)),
        scratch_shapes=[pltpu.VMEM((tm, md), BF16)],
        compiler_params=_params(("arbitrary", "arbitrary")),
        name="mix_out",
    )(h, tok, q_mem, mem_kv, mem_kv, w_out, w_out)


def _compress_kernel(c_ref, pos_ref, w1_ref, w2_ref, o_ref):
    x = c_ref[0]
    nc, half = x.shape
    w1a = w1_ref[0, 0:half, :]
    w1b = w1_ref[0, half:, :]
    pos = jnp.broadcast_to(pos_ref[0], (SUBLANES, 2 * half))
    bias = _dot(pos, w1_ref[0])[0:1, :]
    first = _dot(x, w1a)
    second = pltpu.roll(_dot(x, w1b), nc - 1, axis=0)
    hid = jax.nn.gelu(first + second + bias, approximate=True)
    out = _dot(hid.astype(BF16), w2_ref[0])
    row = lax.broadcasted_iota(jnp.int32, out.shape, 0)
    o_ref[0] = jnp.where(row < nc - 1, out, 0.0).astype(o_ref.dtype)


def _compress(chunks, pos, w1, w2):
    n, nc, cw = chunks.shape
    g = n // 2
    return pl.pallas_call(
        _compress_kernel,
        out_shape=jax.ShapeDtypeStruct((2 * g, nc, HEAD_DIM), BF16),
        grid=(2 * g,),
        in_specs=[
            pl.BlockSpec((1, nc, cw), lambda i: (i, 0, 0)),
            pl.BlockSpec((1, 1, 2 * cw), lambda i: (i // g, 0, 0)),
            pl.BlockSpec((1, 2 * cw, HEAD_DIM), lambda i: (i // g, 0, 0)),
            pl.BlockSpec((1, HEAD_DIM, HEAD_DIM), lambda i: (i // g, 0, 0)),
        ],
        out_specs=pl.BlockSpec((1, nc, HEAD_DIM), lambda i: (i, 0, 0)),
        compiler_params=_params(("arbitrary",)),
        name="compress",
    )(chunks, pos, w1, w2)


def _softmax2_parts(s, bias):
    s = s + bias
    e = jnp.exp2(s - jnp.max(s, axis=-1, keepdims=True))
    return e, 1.0 / jnp.sum(e, axis=-1, keepdims=True)


def _nsa_cmp_kernel(q_ref, gate_ref, kc_ref, vc_ref, ovt_ref, kw_ref, vw_ref, ocw_ref, bias_ref,
                    *, tq, tw, rep, n_pick):
    i = pl.program_id(1)
    t0 = i * tq
    rows = rep * tq
    q3 = q_ref[...].reshape(rows, HEAD_DIM)
    gates = gate_ref[...]

    kc = kc_ref[0]
    n_cmp = kc.shape[0]
    cmp_end = lax.broadcasted_iota(jnp.int32, (tq, n_cmp), 1) * CMP_STRIDE + (CMP_BLOCK - 1)
    t_c = t0 + lax.broadcasted_iota(jnp.int32, (tq, n_cmp), 0)
    bias_c = jnp.where(cmp_end <= t_c, 0.0, NEG_BIG)
    sees_any = t0 + lax.broadcasted_iota(jnp.int32, (tq, 1), 0) >= CMP_BLOCK - 1
    o_c, inv_c, p_sum = [], [], None
    for r in range(rep):
        e_r, inv_r = _softmax2_parts(_dot_nt(q_ref[r], kc), bias_c)
        inv_r = jnp.where(sees_any, inv_r, 0.0)
        o_c.append(_dot(e_r.astype(BF16), vc_ref[0]))
        inv_c.append(inv_r)
        p_r = e_r * inv_r
        p_sum = p_r if p_sum is None else p_sum + p_r

    p_hi = p_sum.astype(BF16)
    p_lo = (p_sum - p_hi.astype(F32)).astype(BF16)
    imp_all = _dot_nt(ovt_ref[...], p_hi) + _dot_nt(ovt_ref[...], p_lo)
    n_slc = imp_all.shape[0]
    blk = lax.broadcasted_iota(jnp.int32, (n_slc, LANES), 0)
    blk_f = blk.astype(F32)
    for c in range(tq // LANES):
        imp = imp_all[:, c * LANES:(c + 1) * LANES]
        jt = (t0 + c * LANES + lax.broadcasted_iota(jnp.int32, imp.shape, 1)) // SLC_BLOCK
        valid = blk <= jt
        forced = (blk == 0) | (blk == jt) | (blk == jt - 1)
        work = jnp.where(valid, jnp.where(forced, -jnp.inf, imp), -jnp.inf)
        sel_bias = jnp.where(forced, 0.0, NEG_BIG)
        for _ in range(n_pick):
            mx = jnp.max(work, axis=0, keepdims=True)
            first = jnp.min(jnp.where(work == mx, blk_f, float(n_slc)), axis=0, keepdims=True)
            pick = blk_f == first
            sel_bias = jnp.where(pick, 0.0, sel_bias)
            work = jnp.where(pick, -jnp.inf, work)
        bias_ref[0, c * LANES:(c + 1) * LANES, :] = jnp.where(valid, sel_bias, NEG_BIG).T.astype(bias_ref.dtype)

    wlen = tw + WINDOW
    rel = lax.broadcasted_iota(jnp.int32, (tw, wlen), 1) - lax.broadcasted_iota(jnp.int32, (tw, wlen), 0)
    for sub in range(tq // tw):
        ts0 = t0 + sub * tw
        ws = pl.multiple_of(jnp.maximum(ts0 - WINDOW, 0), tw)
        qs = jnp.concatenate([q_ref[r, sub * tw:(sub + 1) * tw, :] for r in range(rep)], axis=0)
        s_w = _dot_nt(qs, kw_ref[0, pl.ds(ws, wlen), :])
        rel_w = rel + (ws - ts0)
        bias_w = jnp.where((rel_w <= 0) & (rel_w > -WINDOW), 0.0, NEG_BIG)
        e_w, inv_w = _softmax2_parts(s_w, jnp.concatenate([bias_w] * rep, axis=0))
        o_w = _dot(e_w.astype(BF16), vw_ref[0, pl.ds(ws, wlen), :])
        gs = gates[sub * tw:(sub + 1) * tw]
        for r in range(rep):
            cr = slice(sub * tw, (sub + 1) * tw)
            wr = slice(r * tw, (r + 1) * tw)
            ocw_ref[cr, r * HEAD_DIM:(r + 1) * HEAD_DIM] = (
                (gs[:, 3 * r:3 * r + 1] * inv_c[r][cr]) * o_c[r][cr]
                + (gs[:, 3 * r + 2:3 * r + 3] * inv_w[wr]) * o_w[wr])


def _nsa_cmp(q, gates, cmp_kv, overlap, kw, vw, *, tq=1024, tw=128):
    n_h, s, _ = q.shape
    g = NSA_KV_GROUPS
    rep = n_h // g
    nc = cmp_kv.shape[1]
    n_slc = s // SLC_BLOCK
    tq = min(tq, s)
    n_pick = max(min(N_SELECT, n_slc) - N_FORCED, 0)
    kernel = functools.partial(_nsa_cmp_kernel, tq=tq, tw=tw, rep=rep, n_pick=n_pick)
    return pl.pallas_call(
        kernel,
        out_shape=(jax.ShapeDtypeStruct((s, n_h * HEAD_DIM), F32), jax.ShapeDtypeStruct((g, s, n_slc), BF16)),
        grid=(g, s // tq),
        in_specs=[
            pl.BlockSpec((rep, tq, HEAD_DIM), lambda gi, i: (gi, i, 0)),
            pl.BlockSpec((None, tq, LANES), lambda gi, i: (gi, i, 0)),
            pl.BlockSpec((1, nc, HEAD_DIM), lambda gi, i: (gi, 0, 0)),
            pl.BlockSpec((1, nc, HEAD_DIM), lambda gi, i: (g + gi, 0, 0)),
            pl.BlockSpec((n_slc, nc), lambda gi, i: (0, 0)),
            pl.BlockSpec((1, s, HEAD_DIM), lambda gi, i: (gi, 0, 0)),
            pl.BlockSpec((1, s, HEAD_DIM), lambda gi, i: (gi, 0, 0)),
        ],
        out_specs=(pl.BlockSpec((tq, rep * HEAD_DIM), lambda gi, i: (i, gi)),
                   pl.BlockSpec((1, tq, n_slc), lambda gi, i: (gi, i, 0))),
        compiler_params=_params(("arbitrary", "arbitrary")),
        name="nsa_cmp_win",
    )(q, gates, cmp_kv, cmp_kv, overlap, kw, vw)


def _nsa_sel_kernel(q_ref, bias_ref, ks_ref, vs_ref, ocw_ref, gate_ref, o_ref, m_ref, acc_ref, sa_ref, sb_ref,
                    *, tq, tk, rep):
    i = pl.program_id(1)
    t0 = i * tq
    rows = rep * tq
    q3 = q_ref[...].reshape(rows, HEAD_DIM)
    q_aug = jnp.concatenate([q3, jnp.concatenate([bias_ref[0]] * rep, axis=0)], axis=1)

    m_ref[...] = jnp.full(m_ref.shape, NEG_BIG, F32)
    acc_ref[...] = jnp.zeros(acc_ref.shape, F32)

    def scores(kt, s_ref):
        k_aug = ks_ref[0, pl.ds(pl.multiple_of(kt * tk, tk), tk), :]
        for r in range(rep):
            rs = slice(r * tq, (r + 1) * tq)
            s_ref[rs, :] = _dot_nt(q_aug[rs], k_aug)

    def consume(kt, s_ref, causal):
        k0 = pl.multiple_of(kt * tk, tk)
        v_aug = vs_ref[0, pl.ds(k0, tk), :]
        for r in range(rep):
            rs = slice(r * tq, (r + 1) * tq)
            s = s_ref[rs, :]
            if causal:
                t_s = t0 + lax.broadcasted_iota(jnp.int32, s.shape, 0)
                s = jnp.where(lax.broadcasted_iota(jnp.int32, s.shape, 1) + k0 <= t_s, s, NEG_BIG)
            m_prev = m_ref[rs, :]
            m_new = jnp.maximum(m_prev, jnp.max(s, axis=-1, keepdims=True))
            alpha = jnp.exp2(m_prev - m_new)
            p = jnp.exp2(s - jnp.concatenate([m_new] * (tk // LANES), axis=1))
            acc_ref[rs, :] = (jnp.concatenate([alpha, alpha], axis=1) * acc_ref[rs, :]
                              + _dot(p.astype(BF16), v_aug))
            m_ref[rs, :] = m_new

    last = (t0 + tq + tk - 1) // tk - 1
    n_pairs = last // 2

    scores(0, sa_ref)

    def pair(a):
        kt = 2 * a
        scores(kt + 1, sb_ref)
        consume(kt, sa_ref, False)
        scores(kt + 2, sa_ref)
        consume(kt + 1, sb_ref, False)

    def two_pairs(b, carry):
        pair(2 * b)
        pair(2 * b + 1)
        return carry

    lax.fori_loop(0, n_pairs // 2, two_pairs, 0)

    @pl.when(n_pairs % 2 == 1)
    def _():
        pair(n_pairs - 1)

    @pl.when(last == 2 * n_pairs)
    def _():
        consume(last, sa_ref, True)

    @pl.when(last != 2 * n_pairs)
    def _():
        scores(last, sb_ref)
        consume(last - 1, sa_ref, False)
        consume(last, sb_ref, True)

    acc = acc_ref[...]
    o_s = acc[:, :HEAD_DIM] / acc[:, HEAD_DIM:]
    gates = gate_ref[...]
    for r in range(rep):
        cs = slice(r * HEAD_DIM, (r + 1) * HEAD_DIM)
        o_ref[:, cs] = (ocw_ref[:, cs] + gates[:, 3 * r + 1:3 * r + 2] * o_s[r * tq:(r + 1) * tq]).astype(o_ref.dtype)


def _nsa_sel(q, gates, sel_bias, ks_aug, vs_aug, ocw, *, tq=512, tk=512):
    n_h, s, _ = q.shape
    g = NSA_KV_GROUPS
    rep = n_h // g
    n_slc = sel_bias.shape[2]
    tk = min(tk, s)
    tq = min(tq, tk)
    rows = rep * tq
    kernel = functools.partial(_nsa_sel_kernel, tq=tq, tk=tk, rep=rep)
    return pl.pallas_call(
        kernel,
        out_shape=jax.ShapeDtypeStruct((s, n_h * HEAD_DIM), BF16),
        grid=(g, s // tq),
        in_specs=[
            pl.BlockSpec((rep, tq, HEAD_DIM), lambda gi, i: (gi, i, 0)),
            pl.BlockSpec((1, tq, n_slc), lambda gi, i: (gi, i, 0)),
            pl.BlockSpec((1, s, HEAD_DIM + n_slc), lambda gi, i: (gi, 0, 0)),
            pl.BlockSpec((1, s, 2 * HEAD_DIM), lambda gi, i: (gi, 0, 0)),
            pl.BlockSpec((tq, rep * HEAD_DIM), lambda gi, i: (i, gi)),
            pl.BlockSpec((None, tq, LANES), lambda gi, i: (gi, i, 0)),
        ],
        out_specs=pl.BlockSpec((tq, rep * HEAD_DIM), lambda gi, i: (i, gi)),
        scratch_shapes=[pltpu.VMEM((rows, LANES), F32), pltpu.VMEM((rows, 2 * HEAD_DIM), F32),
                        pltpu.VMEM((rows, tk), F32), pltpu.VMEM((rows, tk), F32)],
        compiler_params=_params(("arbitrary", "arbitrary")),
        name="nsa_sel",
    )(q, sel_bias, ks_aug, vs_aug, ocw, gates)


def _slc_from_cmp(n_slc, n_cmp):
    cs = jnp.arange(n_cmp)[None, :] * CMP_STRIDE
    ss = jnp.arange(n_slc)[:, None] * SLC_BLOCK
    ov = jnp.clip(jnp.minimum(cs + CMP_BLOCK, ss + SLC_BLOCK) - jnp.maximum(cs, ss), 0, None)
    return (ov.astype(F32) / CMP_BLOCK).astype(BF16)


def kernel(x, mem, positions, ffn_norm, ffn_w_gate, ffn_w_up, ffn_w_down, mix_norm, mem_norm, w_mem_kv, w_out, w_in_conv, conv_w, w_in_nsa, kv_norm, w_kv, cmp_pos_k, cmp_w1_k, cmp_w2_k, cmp_pos_v, cmp_w1_v, cmp_w2_v, final_norm):
    b, s, d = x.shape
    assert b == 1
    depth = ffn_norm.shape[0]
    n_a = w_in_conv.shape[0]
    mem_dim = w_mem_kv.shape[2] // 2
    conv_ch = conv_w.shape[2]
    kv_w = w_kv.shape[1] // 6
    g = NSA_KV_GROUPS
    n_q = w_in_nsa.shape[2] - mem_dim
    n_heads = n_q // (HEAD_DIM + 3)
    q_dim = n_heads * HEAD_DIM
    rep = n_heads // g
    n_slc = s // SLC_BLOCK
    bf = lambda w: w.astype(BF16)

    h = x[0]
    mem2 = mem[0]
    tables = _rope_tables(positions.reshape(s, 1))

    kv_side = None
    for layer in range(depth):
        if layer == n_a:
            chunks, ks_aug, vs_aug, kw, vw = _kv_side(h, kv_norm, w_kv, tables)
            nc = s // CMP_STRIDE
            pos = bf(jnp.stack([cmp_pos_k, cmp_pos_v]).reshape(2, 1, CMP_BLOCK * HEAD_DIM))
            cmp_kv = _compress(chunks, pos, bf(jnp.stack([cmp_w1_k, cmp_w1_v])), bf(jnp.stack([cmp_w2_k, cmp_w2_v])))
            kv_side = (cmp_kv, ks_aug, vs_aug, kw, vw, _slc_from_cmp(n_slc, nc))

        h = _ffn(h, ffn_norm[layer, 0], ffn_w_gate, ffn_w_up, ffn_w_down, layer, 0)

        if layer < n_a:
            tok, q_mem = _mix_a(h, mix_norm[layer], w_in_conv, conv_w, layer)
        else:
            w_t = jnp.transpose(w_in_nsa, (0, 2, 1))
            w_tail = _nsa_tail(w_t, layer - n_a, q_dim=q_dim, n_gate=3 * n_heads, mem_dim=mem_dim)
            q, q_mem, gates = _mix_b(h, mix_norm[layer], w_t, w_tail, tables, layer - n_a, q_dim=q_dim,
                                     mem_dim=mem_dim, scale=HEAD_DIM ** -0.5 * math.log2(math.e),
                                     n_grp=g, grp_gates=3 * rep)
            cmp_kv, ks_aug, vs_aug, kw, vw, overlap = kv_side
            ocw, sel_bias = _nsa_cmp(q, gates, cmp_kv, overlap, kw, vw)
            tok = _nsa_sel(q, gates, sel_bias, ks_aug, vs_aug, ocw)

        h = _mix_out(h, tok, q_mem, _mem_kv(mem2, mem_norm[layer], w_mem_kv, layer), w_out, layer)

        last = layer == depth - 1
        h = _ffn(h, ffn_norm[layer, 1], ffn_w_gate, ffn_w_up, ffn_w_down, layer, 1, final_norm if last else None)

    return h[None]
```
